```python
import math
import jax, jax.numpy as jnp
from jax import lax
import numpy as np

D_MODEL = 1024
BATCH = 16
SEQ = 2048
DEPTH = 2

N_MIXERS = 2
EPS = 1e-6
POOL_WINDOWS = (2, 4, 8, 16)
POOL_GROUPS = 4
POOL_GC = D_MODEL // POOL_GROUPS
MLSTM_HEADS = 8
MLSTM_DV = D_MODEL // MLSTM_HEADS
MLSTM_DQK = MLSTM_DV // 2
QK_W = MLSTM_HEADS * MLSTM_DQK
V_W = MLSTM_HEADS * MLSTM_DV
IN_W = 2 * QK_W + 2 * V_W + 2 * MLSTM_HEADS
CONV_K = 4
CHUNK = 64
N_GROUPS = 4
EXPERTS_PER_GROUP = 4
N_EXPERTS = N_GROUPS * EXPERTS_PER_GROUP
TOP_K = 2
D_FF_EXPERT = D_MODEL // 2
N_POOL = (DEPTH + N_MIXERS - 1) // N_MIXERS
N_MLSTM = DEPTH // N_MIXERS

kernel_name = "hybrid_pool_mlstm_hmoe"


def rmsnorm(x, g):
    xf = x.astype(jnp.float32)
    y = xf * lax.rsqrt(jnp.mean(xf * xf, axis=-1, keepdims=True) + EPS)
    return (y * g.astype(jnp.float32)).astype(x.dtype)


def pool_mixer(xn, w_pool, scale):
    B, S, D = xn.shape
    xf = xn.astype(jnp.float32)
    cs = jnp.pad(jnp.cumsum(xf, axis=1), ((0, 0), (1, 0), (0, 0)))
    t = jnp.arange(S, dtype=jnp.float32)
    outs = []
    for g, w in enumerate(POOL_WINDOWS):
        sl = slice(g * POOL_GC, (g + 1) * POOL_GC)
        c = cs[..., sl]
        upper = c[:, 1:]
        lower = jnp.pad(c[:, :S + 1 - w], ((0, 0), (w - 1, 0), (0, 0)))
        count = jnp.minimum(t + 1.0, float(w))
        outs.append((upper - lower) / count[None, :, None] - xf[..., sl])
    pooled = jnp.stack(outs, axis=2).astype(xn.dtype)
    y = jnp.einsum('bsgc,gcd->bsgd', pooled, w_pool).reshape(B, S, D)
    return y * scale


def causal_conv(x, w, b):
    C = x.shape[-1]
    y = lax.conv_general_dilated(x, w[:, None, :].astype(x.dtype), window_strides=(1,),
                                 padding=[(CONV_K - 1, 0)],
                                 dimension_numbers=('NWC', 'WIO', 'NWC'),
                                 feature_group_count=C)
    return y + b.astype(x.dtype)


def mlstm_chunkwise(q, k, v, li, lf):
    B, S, H, dk = q.shape
    dv = v.shape[-1]
    L = CHUNK
    NC = S // L

    def chunks4(a):
        return a.reshape(B, NC, L, H, a.shape[-1]).transpose(1, 0, 3, 2, 4)

    def chunks3(a):
        return a.reshape(B, NC, L, H).transpose(1, 0, 3, 2)

    mask = jnp.tril(jnp.ones((L, L), dtype=bool))

    def step(carry, xs):
        C, n, m = carry
        qc, kc, vc, lic, lfc = xs
        b = jnp.cumsum(lfc, axis=-1)
        dmat = jnp.where(mask, b[..., :, None] - b[..., None, :] + lic[..., None, :], -jnp.inf)
        inter = b + m[..., None]
        m_t = jnp.maximum(inter, jnp.max(dmat, axis=-1))
        scores = jnp.einsum('bhtd,bhsd->bhts', qc, kc) * jnp.exp(dmat - m_t[..., None])
        s_inter = jnp.exp(inter - m_t)
        num = (jnp.einsum('bhts,bhsv->bhtv', scores, vc)
               + s_inter[..., None] * jnp.einsum('bhvd,bhtd->bhtv', C, qc))
        den = jnp.sum(scores, axis=-1) + s_inter * jnp.einsum('bhd,bhtd->bht', n, qc)
        h = num / jnp.maximum(jnp.abs(den), jnp.exp(-m_t))[..., None]
        bL = b[..., -1]
        g = bL[..., None] - b + lic
        m_new = jnp.maximum(bL + m, jnp.max(g, axis=-1))
        w_s = jnp.exp(g - m_new[..., None])
        decay = jnp.exp(bL + m - m_new)
        C_new = decay[..., None, None] * C + jnp.einsum('bhs,bhsv,bhsd->bhvd', w_s, vc, kc)
        n_new = decay[..., None] * n + jnp.einsum('bhs,bhsd->bhd', w_s, kc)
        return (C_new, n_new, m_new), h

    init = (jnp.zeros((B, H, dv, dk), jnp.float32), jnp.zeros((B, H, dk), jnp.float32),
            jnp.zeros((B, H), jnp.float32))
    xs = (chunks4(q), chunks4(k), chunks4(v), chunks3(li), chunks3(lf))
    _, hs = lax.scan(step, init, xs)
    return hs.transpose(1, 0, 3, 2, 4).reshape(B, S, H, dv)


def mlstm_mixer(xn, w_in, conv_w, conv_b, b_i, b_f, head_norm, w_out):
    B, S, _ = xn.shape
    H = MLSTM_HEADS
    proj = xn @ w_in
    qk_pre, v, o_pre, gates = jnp.split(proj, [2 * QK_W, 2 * QK_W + V_W, 2 * QK_W + 2 * V_W], axis=-1)
    qk = jax.nn.silu(causal_conv(qk_pre, conv_w, conv_b))
    q = qk[..., :QK_W].reshape(B, S, H, MLSTM_DQK).astype(jnp.float32)
    k = qk[..., QK_W:].reshape(B, S, H, MLSTM_DQK).astype(jnp.float32) * (MLSTM_DQK ** -0.5)
    v = v.reshape(B, S, H, MLSTM_DV).astype(jnp.float32)
    gates = gates.astype(jnp.float32)
    li = gates[..., :H] + b_i.astype(jnp.float32)
    lf = jax.nn.log_sigmoid(gates[..., H:] + b_f.astype(jnp.float32))
    h = mlstm_chunkwise(q, k, v, li, lf)
    h = h * lax.rsqrt(jnp.mean(h * h, axis=-1, keepdims=True) + EPS)
    h = h.reshape(B, S, V_W) * head_norm.astype(jnp.float32)
    h = h * jax.nn.sigmoid(o_pre.astype(jnp.float32))
    return h.astype(xn.dtype) @ w_out


def hier_moe(xn, w_gr, b_gr, w_er, b_er, w_gate, w_up, w_down):
    B, S, D = xn.shape
    T = B * S
    xt = xn.reshape(T, D)
    gl = (xt @ w_gr).astype(jnp.float32) + b_gr.astype(jnp.float32)
    gp = jax.nn.softmax(gl, axis=-1)
    gsel = jnp.argmax(gl, axis=-1)
    pg = jnp.take_along_axis(gp, gsel[:, None], axis=-1)[:, 0]
    el = ((xt @ w_er).astype(jnp.float32) + b_er.astype(jnp.float32)).reshape(T, N_GROUPS, EXPERTS_PER_GROUP)
    el_sel = jnp.take_along_axis(el, gsel[:, None, None], axis=1)[:, 0]
    vals, idx = lax.top_k(el_sel, TOP_K)
    wk = jax.nn.softmax(vals, axis=-1) * pg[:, None]
    eid = gsel[:, None] * EXPERTS_PER_GROUP + idx
    comb = jnp.sum(jax.nn.one_hot(eid, N_EXPERTS, dtype=jnp.float32) * wk[..., None], axis=1)
    comb = comb.astype(xn.dtype)
    y = jnp.zeros((T, D), xn.dtype)
    for g in range(N_GROUPS):
        es = slice(g * EXPERTS_PER_GROUP, (g + 1) * EXPERTS_PER_GROUP)
        hg = jnp.einsum('td,edf->tef', xt, w_gate[es])
        hu = jnp.einsum('td,edf->tef', xt, w_up[es])
        act = jax.nn.silu(hg) * hu * comb[:, es, None]
        y = y + jnp.einsum('tef,efd->td', act, w_down[es])
    return y.reshape(B, S, D)


def setup_inputs(seed: int = 0) -> dict:
    key = jax.random.key(seed)
    ks = jax.random.split(key, 20)
    D = D_MODEL
    f32 = jnp.float32
    nrm = lambda k, shape, s: jax.random.normal(k, shape, f32) * s
    b_f = (jnp.tile(jnp.linspace(3.0, 6.0, MLSTM_HEADS, dtype=f32)[None], (N_MLSTM, 1))
           + nrm(ks[8], (N_MLSTM, MLSTM_HEADS), 0.1))
    return {
        "x": nrm(ks[0], (BATCH, SEQ, D), 1.0),
        "mix_norm": 1.0 + nrm(ks[1], (DEPTH, D), 0.02),
        "pool_w": nrm(ks[2], (N_POOL, POOL_GROUPS, POOL_GC, POOL_GC), POOL_GC ** -0.5),
        "pool_scale": 1.0 + nrm(ks[3], (N_POOL, D), 0.02),
        "w_in": nrm(ks[4], (N_MLSTM, D, IN_W), D ** -0.5),
        "conv_w": nrm(ks[5], (N_MLSTM, CONV_K, 2 * QK_W), CONV_K ** -0.5),
        "conv_b": nrm(ks[6], (N_MLSTM, 2 * QK_W), 0.01),
        "gate_bias_i": nrm(ks[7], (N_MLSTM, MLSTM_HEADS), 0.01),
        "gate_bias_f": b_f,
        "head_norm": 1.0 + nrm(ks[9], (N_MLSTM, V_W), 0.02),
        "w_out": nrm(ks[10], (N_MLSTM, V_W, D), V_W ** -0.5),
        "ffn_norm": 1.0 + nrm(ks[11], (DEPTH, D), 0.02),
        "w_group_router": nrm(ks[12], (DEPTH, D, N_GROUPS), D ** -0.5),
        "b_group_router": nrm(ks[13], (DEPTH, N_GROUPS), 0.01),
        "w_expert_router": nrm(ks[14], (DEPTH, D, N_EXPERTS), D ** -0.5),
        "b_expert_router": nrm(ks[15], (DEPTH, N_EXPERTS), 0.01),
        "w_gate": nrm(ks[16], (DEPTH, N_EXPERTS, D, D_FF_EXPERT), D ** -0.5),
        "w_up": nrm(ks[17], (DEPTH, N_EXPERTS, D, D_FF_EXPERT), D ** -0.5),
        "w_down": nrm(ks[18], (DEPTH, N_EXPERTS, D_FF_EXPERT, D), D_FF_EXPERT ** -0.5),
        "final_norm": 1.0 + nrm(ks[19], (D,), 0.02),
    }


def reference(x, mix_norm, pool_w, pool_scale, w_in, conv_w, conv_b, gate_bias_i, gate_bias_f,
              head_norm, w_out, ffn_norm, w_group_router, b_group_router, w_expert_router,
              b_expert_router, w_gate, w_up, w_down, final_norm):
    h = x
    for i in range(DEPTH):
        j = i // N_MIXERS
        hn = rmsnorm(h, mix_norm[i])
        if i % N_MIXERS == 0:
            h = h + pool_mixer(hn, pool_w[j], pool_scale[j])
        else:
            h = h + mlstm_mixer(hn, w_in[j], conv_w[j], conv_b[j], gate_bias_i[j], gate_bias_f[j],
                                head_norm[j], w_out[j])
        hn = rmsnorm(h, ffn_norm[i])
        h = h + hier_moe(hn, w_group_router[i], b_group_router[i], w_expert_router[i],
                         b_expert_router[i], w_gate[i], w_up[i], w_down[i])
    return rmsnorm(h, final_norm)
```

```python
import functools

import jax
import jax.numpy as jnp
from jax import lax
from jax.experimental import pallas as pl
from jax.experimental.pallas import tpu as pltpu

EPS = 1e-6
POOL_WINDOWS = (2, 4, 8, 16)
MLSTM_HEADS = 8
CONV_K = 4
N_GROUPS = 4
EXPERTS_PER_GROUP = 4
N_EXPERTS = N_GROUPS * EXPERTS_PER_GROUP

SUBLANES = 8
LANES = 128
VMEM_LIMIT_BYTES = 56 * 1024 * 1024

POOL_SEQ_TILE = 512
POOL_HALO = 16
MOE_TOKEN_TILE = 1024
MLSTM_CHUNK = 256
CONV_CARRY = SUBLANES

F32 = jnp.float32
BF16 = jnp.bfloat16
HIGHEST = lax.Precision.HIGHEST


def _rmsnorm(v, g):
    return (v * lax.rsqrt(jnp.mean(v * v, axis=-1, keepdims=True) + EPS)) * g


def _sigmoid(v):
    return 1.0 / (1.0 + jnp.exp(-v))


def _log_sigmoid(v):
    return jnp.minimum(v, 0.0) - jnp.log1p(jnp.exp(-jnp.abs(v)))


def _pool_kernel(x_ref, halo_ref, g_ref, w_ref, scale_ref, o_ref):
    j = pl.program_id(1)
    ts = x_ref.shape[1]
    gc = w_ref.shape[1]
    x = x_ref[0]
    g = g_ref[...]
    hn = _rmsnorm(x, g)
    hh = _rmsnorm(halo_ref[0], g) * (j > 0).astype(F32)
    s = jnp.concatenate([hh, hn], axis=0)
    sums = []
    shift = 1
    for _ in POOL_WINDOWS:
        s = s + pltpu.roll(s, shift, 0)
        sums.append(s[POOL_HALO:, :gc])
        s = s[:, gc:]
        shift *= 2
    t = (j * ts + lax.broadcasted_iota(jnp.int32, (ts, 1), 0) + 1).astype(F32)
    ys = []
    for k, w in enumerate(POOL_WINDOWS):
        count = jnp.minimum(t, float(w))
        pooled = sums[k] / count - hn[:, k * gc:(k + 1) * gc]
        ys.append(jnp.dot(pooled.astype(BF16), w_ref[k], preferred_element_type=F32))
    y = jnp.concatenate(ys, axis=-1)
    o_ref[0] = x + y * scale_ref[...]


def _pool_layer(x, g, w, scale):
    B, S, D = x.shape
    ts = min(POOL_SEQ_TILE, S)
    assert S % ts == 0 and ts % POOL_HALO == 0
    n_groups, gc, _ = w.shape
    assert n_groups == len(POOL_WINDOWS) and n_groups * gc == D
    halo_blocks = ts // POOL_HALO
    return pl.pallas_call(
        _pool_kernel,
        grid=(B, S // ts),
        in_specs=[
            pl.BlockSpec((1, ts, D), lambda b, j: (b, j, 0)),
            pl.BlockSpec((1, POOL_HALO, D), lambda b, j: (b, jnp.maximum(j * halo_blocks - 1, 0), 0)),
            pl.BlockSpec((1, D), lambda b, j: (0, 0)),
            pl.BlockSpec((n_groups, gc, gc), lambda b, j: (0, 0, 0)),
            pl.BlockSpec((1, D), lambda b, j: (0, 0)),
        ],
        out_specs=pl.BlockSpec((1, ts, D), lambda b, j: (b, j, 0)),
        out_shape=jax.ShapeDtypeStruct((B, S, D), F32),
        compiler_params=pltpu.CompilerParams(
            dimension_semantics=("arbitrary", "arbitrary"), vmem_limit_bytes=VMEM_LIMIT_BYTES),
        name="pool_mixer",
    )(x, x, g.reshape(1, D), w.astype(BF16), scale.reshape(1, D))


def _route(hn, wgr, bgr, wer, ber):
    gl = jnp.dot(hn, wgr, preferred_element_type=F32, precision=HIGHEST) + bgr
    el = jnp.dot(hn, wer, preferred_element_type=F32, precision=HIGHEST) + ber
    n_g = gl.shape[-1]
    n_e = el.shape[-1]
    gidx = lax.broadcasted_iota(jnp.int32, gl.shape, 1)
    gmax = jnp.max(gl, axis=-1, keepdims=True)
    gsel = jnp.min(jnp.where(gl == gmax, gidx, n_g), axis=-1, keepdims=True)
    pg = 1.0 / jnp.sum(jnp.exp(gl - gmax), axis=-1, keepdims=True)
    eidx = lax.broadcasted_iota(jnp.int32, el.shape, 1)
    in_group = (eidx // (n_e // n_g)) == gsel
    neg_inf = jnp.float32(-jnp.inf)
    cand = jnp.where(in_group, el, neg_inf)
    v1 = jnp.max(cand, axis=-1, keepdims=True)
    i1 = jnp.min(jnp.where(cand == v1, eidx, n_e), axis=-1, keepdims=True)
    cand2 = jnp.where(eidx == i1, neg_inf, cand)
    v2 = jnp.max(cand2, axis=-1, keepdims=True)
    i2 = jnp.min(jnp.where(cand2 == v2, eidx, n_e), axis=-1, keepdims=True)
    r = jnp.exp(v2 - v1)
    w1 = pg / (1.0 + r)
    w2 = pg * r / (1.0 + r)
    return jnp.where(eidx == i1, w1, 0.0) + jnp.where(eidx == i2, w2, 0.0)


def _moe_kernel(h_ref, g_ref, wgr_ref, bgr_ref, wer_ref, ber_ref, wg_ref, wu_ref, wd_ref, fn_ref,
                o_ref, hn_ref, comb_ref, *, final_norm):
    e = pl.program_id(1)

    @pl.when(e == 0)
    def _():
        x = h_ref[...]
        hn = _rmsnorm(x, g_ref[...])
        comb_ref[...] = _route(hn, wgr_ref[...], bgr_ref[...], wer_ref[...], ber_ref[...])
        hn_ref[...] = hn.astype(BF16)
        o_ref[...] = x

    hn = hn_ref[...]
    comb = comb_ref[...]
    eidx = lax.broadcasted_iota(jnp.int32, comb.shape, 1)
    comb_e = jnp.sum(jnp.where(eidx == e, comb, 0.0), axis=-1, keepdims=True)
    hg = jnp.dot(hn, wg_ref[0], preferred_element_type=F32)
    hu = jnp.dot(hn, wu_ref[0], preferred_element_type=F32)
    act = (hg * _sigmoid(hg)) * hu * comb_e
    o_ref[...] += jnp.dot(act.astype(BF16), wd_ref[0], preferred_element_type=F32)

    if final_norm:
        @pl.when(e == pl.num_programs(1) - 1)
        def _():
            o_ref[...] = _rmsnorm(o_ref[...], fn_ref[...])


def _moe_layer(h, g, wgr, bgr, wer, ber, wg, wu, wd, fn=None):
    T, D = h.shape
    n_e, _, F = wg.shape
    n_g = wgr.shape[-1]
    tm = min(MOE_TOKEN_TILE, T)
    assert T % tm == 0
    final_norm = fn is not None
    if fn is None:
        fn = jnp.ones((D,), F32)
    const2 = lambda i, e: (0, 0)
    return pl.pallas_call(
        functools.partial(_moe_kernel, final_norm=final_norm),
        grid=(T // tm, n_e),
        in_specs=[
            pl.BlockSpec((tm, D), lambda i, e: (i, 0)),
            pl.BlockSpec((1, D), const2),
            pl.BlockSpec((D, n_g), const2),
            pl.BlockSpec((1, n_g), const2),
            pl.BlockSpec((D, n_e), const2),
            pl.BlockSpec((1, n_e), const2),
            pl.BlockSpec((1, D, F), lambda i, e: (e, 0, 0)),
            pl.BlockSpec((1, D, F), lambda i, e: (e, 0, 0)),
            pl.BlockSpec((1, F, D), lambda i, e: (e, 0, 0)),
            pl.BlockSpec((1, D), const2),
        ],
        out_specs=pl.BlockSpec((tm, D), lambda i, e: (i, 0)),
        out_shape=jax.ShapeDtypeStruct((T, D), F32),
        scratch_shapes=[pltpu.VMEM((tm, D), BF16), pltpu.VMEM((tm, n_e), F32)],
        compiler_params=pltpu.CompilerParams(
            dimension_semantics=("arbitrary", "arbitrary"), vmem_limit_bytes=VMEM_LIMIT_BYTES),
        name="moe_final" if final_norm else "moe",
    )(h, g.reshape(1, D), wgr, bgr.reshape(1, n_g), wer, ber.reshape(1, n_e),
      wg.astype(BF16), wu.astype(BF16), wd.astype(BF16), fn.reshape(1, D))


def _mlstm_kernel(h_ref, g_ref, wqk_ref, wv_ref, wo_ref, wgate_ref, wgate_t_ref, convw_ref, convb_ref,
                  bi_ref, bf_ref, bi_t_ref, bf_t_ref, hnorm_ref, wout_ref, o_ref,
                  carry_ref, c_ref, n_ref, m_ref):
    j = pl.program_id(1)
    L = h_ref.shape[1]
    H = MLSTM_HEADS
    qk_w = wqk_ref.shape[1] // 2
    dqk = qk_w // H
    dv = wv_ref.shape[1] // H
    assert 2 * dqk == LANES and dv == LANES

    @pl.when(j == 0)
    def _():
        carry_ref[...] = jnp.zeros_like(carry_ref)
        c_ref[...] = jnp.zeros_like(c_ref)
        n_ref[...] = jnp.zeros_like(n_ref)
        m_ref[...] = jnp.zeros_like(m_ref)

    x = h_ref[0]
    hn = _rmsnorm(x, g_ref[...])
    hb = hn.astype(BF16)
    qk_pre = jnp.dot(hb, wqk_ref[...], preferred_element_type=F32)
    v = jnp.dot(hb, wv_ref[...], preferred_element_type=F32)
    o_pre = jnp.dot(hb, wo_ref[...], preferred_element_type=F32)
    gates = jnp.dot(hn, wgate_ref[...], preferred_element_type=F32, precision=HIGHEST)
    gates_t = lax.dot_general(wgate_t_ref[...], hn, (((1,), (1,)), ((), ())),
                              preferred_element_type=F32, precision=HIGHEST)

    ext = jnp.concatenate([carry_ref[...], qk_pre], axis=0)
    carry_ref[...] = qk_pre[L - CONV_CARRY:, :]
    conv = ext * convw_ref[CONV_K - 1:CONV_K, :]
    for d in range(1, CONV_K):
        conv = conv + pltpu.roll(ext, d, 0) * convw_ref[CONV_K - 1 - d:CONV_K - d, :]
    qk = conv[CONV_CARRY:, :] + convb_ref[...]
    qk = qk * _sigmoid(qk)
    q = qk[:, :qk_w]
    k = qk[:, qk_w:] * (dqk ** -0.5)
    qb = q.astype(BF16)
    kb = k.astype(BF16)
    vb = v.astype(BF16)

    li = gates[:, :H] + bi_ref[...]
    lf = _log_sigmoid(gates[:, H:] + bf_ref[...])
    li_t = gates_t[:H, :] + bi_t_ref[...]
    lf_t = _log_sigmoid(gates_t[H:, :] + bf_t_ref[...])
    row = lax.broadcasted_iota(jnp.int32, (L, L), 0)
    col = lax.broadcasted_iota(jnp.int32, (L, L), 1)
    causal = row >= col
    b = jnp.dot(causal.astype(F32), lf, preferred_element_type=F32, precision=HIGHEST)
    b_t = jnp.dot(lf_t, (row <= col).astype(F32), preferred_element_type=F32, precision=HIGHEST)

    m_prev = m_ref[...]
    inter = b + m_prev
    b_last = b[L - 1:L, :]
    gsum = b_last - b + li
    m_new = jnp.maximum(b_last + m_prev, jnp.max(gsum, axis=0, keepdims=True))
    w_s = jnp.exp(gsum - m_new)
    decay = jnp.exp(b_last + m_prev - m_new)
    m_ref[...] = m_new

    lane = lax.broadcasted_iota(jnp.int32, (1, LANES), 1)
    first = lane < dqk
    prow = lax.broadcasted_iota(jnp.int32, (LANES, 1), 0) < dqk
    neg_inf = jnp.float32(-jnp.inf)
    outs = []
    for p in range(H // 2):
        sl = slice(p * LANES, (p + 1) * LANES)
        q2, k2, q2f, k2f = qb[:, sl], kb[:, sl], q[:, sl], k[:, sl]
        c_pair = c_ref[p]
        c_pair_b = c_pair.astype(BF16)
        n_pair = n_ref[:, sl]
        upd = jnp.zeros_like(c_pair)
        for half in range(2):
            hh = 2 * p + half
            own = first if half == 0 else jnp.logical_not(first)
            qh = jnp.where(own, q2, jnp.zeros_like(q2))
            kh = jnp.where(own, k2, jnp.zeros_like(k2))
            vh = vb[:, hh * dv:(hh + 1) * dv]
            dm = jnp.where(causal, b[:, hh:hh + 1] - (b_t[hh:hh + 1, :] - li_t[hh:hh + 1, :]), neg_inf)
            m_t = jnp.maximum(inter[:, hh:hh + 1], jnp.max(dm, axis=-1, keepdims=True))
            s = lax.dot_general(q2, kh, (((1,), (1,)), ((), ())), preferred_element_type=F32)
            sc = s * jnp.exp(dm - m_t)
            s_inter = jnp.exp(inter[:, hh:hh + 1] - m_t)
            num = (jnp.dot(sc.astype(BF16), vh, preferred_element_type=F32)
                   + s_inter * jnp.dot(qh, c_pair_b, preferred_element_type=F32))
            qn = jnp.sum(jnp.where(own, q2f * n_pair, 0.0), axis=-1, keepdims=True)
            den = jnp.sum(sc, axis=-1, keepdims=True) + s_inter * qn
            ho = num / jnp.maximum(jnp.abs(den), jnp.exp(-m_t))
            outs.append(ho * lax.rsqrt(jnp.mean(ho * ho, axis=-1, keepdims=True) + EPS))
            wsv = (w_s[:, hh:hh + 1] * v[:, hh * dv:(hh + 1) * dv]).astype(BF16)
            upd = upd + lax.dot_general(kh, wsv, (((0,), (0,)), ((), ())), preferred_element_type=F32)
        d0, d1 = decay[:, 2 * p:2 * p + 1], decay[:, 2 * p + 1:2 * p + 2]
        c_ref[p] = jnp.where(prow, d0, d1) * c_pair + upd
        ws_pair = jnp.where(first, w_s[:, 2 * p:2 * p + 1], w_s[:, 2 * p + 1:2 * p + 2])
        n_ref[:, sl] = jnp.where(first, d0, d1) * n_pair + jnp.sum(k2f * ws_pair, axis=0, keepdims=True)

    hcat = jnp.concatenate(outs, axis=-1) * hnorm_ref[...] * _sigmoid(o_pre)
    o_ref[0] = x + jnp.dot(hcat.astype(BF16), wout_ref[...], preferred_element_type=F32)


def _mlstm_layer(h, g, w_in, conv_w, conv_b, b_i, b_f, head_norm, w_out):
    B, S, D = h.shape
    H = MLSTM_HEADS
    v_w = w_out.shape[0]
    qk2 = conv_w.shape[1]
    assert w_in.shape[1] == qk2 + 2 * v_w + 2 * H
    L = min(MLSTM_CHUNK, S)
    assert S % L == 0
    w_qk = w_in[:, :qk2].astype(BF16)
    w_v = w_in[:, qk2:qk2 + v_w].astype(BF16)
    w_o = w_in[:, qk2 + v_w:qk2 + 2 * v_w].astype(BF16)
    w_gate = w_in[:, qk2 + 2 * v_w:]
    const2 = lambda b, j: (0, 0)
    full = lambda a: pl.BlockSpec(a.shape, const2)
    operands = [g.reshape(1, D), w_qk, w_v, w_o, w_gate, w_gate.T, conv_w, conv_b.reshape(1, qk2),
                b_i.reshape(1, H), b_f.reshape(1, H), b_i.reshape(H, 1), b_f.reshape(H, 1),
                head_norm.reshape(1, v_w), w_out.astype(BF16)]
    return pl.pallas_call(
        _mlstm_kernel,
        grid=(B, S // L),
        in_specs=[pl.BlockSpec((1, L, D), lambda b, j: (b, j, 0))] + [full(a) for a in operands],
        out_specs=pl.BlockSpec((1, L, D), lambda b, j: (b, j, 0)),
        out_shape=jax.ShapeDtypeStruct((B, S, D), F32),
        scratch_shapes=[
            pltpu.VMEM((CONV_CARRY, qk2), F32),
            pltpu.VMEM((H // 2, LANES, v_w // H), F32),
            pltpu.VMEM((1, qk2 // 2), F32),
            pltpu.VMEM((1, H), F32),
        ],
        compiler_params=pltpu.CompilerParams(
            dimension_semantics=("arbitrary", "arbitrary"), vmem_limit_bytes=VMEM_LIMIT_BYTES),
        name="mlstm_mixer",
    )(h, *operands)


def kernel(x, mix_norm, pool_w, pool_scale, w_in, conv_w, conv_b, gate_bias_i, gate_bias_f, head_norm,
           w_out, ffn_norm, w_group_router, b_group_router, w_expert_router, b_expert_router, w_gate,
           w_up, w_down, final_norm):
    B, S, D = x.shape
    depth = mix_norm.shape[0]
    n_mixers = 2
    h = x
    for i in range(depth):
        j = i // n_mixers
        if i % n_mixers == 0:
            h = _pool_layer(h, mix_norm[i], pool_w[j], pool_scale[j])
        else:
            h = _mlstm_layer(h, mix_norm[i], w_in[j], conv_w[j], conv_b[j], gate_bias_i[j],
                             gate_bias_f[j], head_norm[j], w_out[j])
        fn = final_norm if i == depth - 1 else None
        h = _moe_layer(h.reshape(B * S, D), ffn_norm[i], w_group_router[i], b_group_router[i],
                       w_expert_router[i], b_expert_router[i], w_gate[i], w_up[i], w_down[i],
                       fn).reshape(B, S, D)
    return h
```

```python
import functools

import jax
import jax.numpy as jnp
from jax import lax
from jax.experimental import pallas as pl
from jax.experimental.pallas import tpu as pltpu

EPS = 1e-6
POOL_WINDOWS = (2, 4, 8, 16)
MLSTM_HEADS = 8
CONV_K = 4
N_GROUPS = 4
EXPERTS_PER_GROUP = 4
N_EXPERTS = N_GROUPS * EXPERTS_PER_GROUP

SUBLANES = 8
LANES = 128
VMEM_LIMIT_BYTES = 56 * 1024 * 1024

POOL_SEQ_TILE = 512
POOL_HALO = 16
MOE_TOKEN_TILE = 1024
MLSTM_CHUNK = 256
MLSTM_BATCH_PER_STEP = 2
CONV_CARRY = SUBLANES

F32 = jnp.float32
BF16 = jnp.bfloat16
HIGHEST = lax.Precision.HIGHEST


def _rmsnorm(v, g):
    return (v * lax.rsqrt(jnp.mean(v * v, axis=-1, keepdims=True) + EPS)) * g


def _sigmoid(v):
    return 1.0 / (1.0 + jnp.exp(-v))


def _log_sigmoid(v):
    return jnp.minimum(v, 0.0) - jnp.log1p(jnp.exp(-jnp.abs(v)))


def _pool_kernel(x_ref, halo_ref, g_ref, w_ref, scale_ref, o_ref):
    j = pl.program_id(1)
    ts = x_ref.shape[1]
    gc = w_ref.shape[1]
    x = x_ref[0]
    g = g_ref[...]
    hn = _rmsnorm(x, g)
    hh = _rmsnorm(halo_ref[0], g) * (j > 0).astype(F32)
    s = jnp.concatenate([hh, hn], axis=0)
    sums = []
    shift = 1
    for _ in POOL_WINDOWS:
        s = s + pltpu.roll(s, shift, 0)
        sums.append(s[POOL_HALO:, :gc])
        s = s[:, gc:]
        shift *= 2
    t = (j * ts + lax.broadcasted_iota(jnp.int32, (ts, 1), 0) + 1).astype(F32)
    ys = []
    for k, w in enumerate(POOL_WINDOWS):
        count = jnp.minimum(t, float(w))
        pooled = sums[k] / count - hn[:, k * gc:(k + 1) * gc]
        ys.append(jnp.dot(pooled.astype(BF16), w_ref[k], preferred_element_type=F32))
    y = jnp.concatenate(ys, axis=-1)
    o_ref[0] = x + y * scale_ref[...]


def _pool_layer(x, g, w, scale):
    B, S, D = x.shape
    ts = min(POOL_SEQ_TILE, S)
    assert S % ts == 0 and ts % POOL_HALO == 0
    n_groups, gc, _ = w.shape
    assert n_groups == len(POOL_WINDOWS) and n_groups * gc == D
    halo_blocks = ts // POOL_HALO
    return pl.pallas_call(
        _pool_kernel,
        grid=(B, S // ts),
        in_specs=[
            pl.BlockSpec((1, ts, D), lambda b, j: (b, j, 0)),
            pl.BlockSpec((1, POOL_HALO, D), lambda b, j: (b, jnp.maximum(j * halo_blocks - 1, 0), 0)),
            pl.BlockSpec((1, D), lambda b, j: (0, 0)),
            pl.BlockSpec((n_groups, gc, gc), lambda b, j: (0, 0, 0)),
            pl.BlockSpec((1, D), lambda b, j: (0, 0)),
        ],
        out_specs=pl.BlockSpec((1, ts, D), lambda b, j: (b, j, 0)),
        out_shape=jax.ShapeDtypeStruct((B, S, D), F32),
        compiler_params=pltpu.CompilerParams(
            dimension_semantics=("arbitrary", "arbitrary"), vmem_limit_bytes=VMEM_LIMIT_BYTES),
        name="pool_mixer",
    )(x, x, g.reshape(1, D), w.astype(BF16), scale.reshape(1, D))


def _split_bf16(a, parts):
    out = []
    for _ in range(parts - 1):
        hi = a.astype(BF16)
        out.append(hi)
        a = a - hi.astype(F32)
    out.append(a.astype(BF16))
    return out


def _split_dot(a_hi, a_lo, w_cat, w_hi):
    n = w_hi.shape[-1]
    p = jnp.dot(a_hi, w_cat, preferred_element_type=F32)
    return p[:, :n] + p[:, n:] + jnp.dot(a_lo, w_hi, preferred_element_type=F32)


def _route(h_hi, h_lo, wr_cat, wr_hi, bgr, ber):
    n_e = ber.shape[-1]
    logits = _split_dot(h_hi, h_lo, wr_cat, wr_hi)
    el = logits[:, :n_e] + ber
    gl = logits[:, n_e:] + bgr
    n_g = gl.shape[-1]
    n_e = el.shape[-1]
    gidx = lax.broadcasted_iota(jnp.int32, gl.shape, 1)
    gmax = jnp.max(gl, axis=-1, keepdims=True)
    gsel = jnp.min(jnp.where(gl == gmax, gidx, n_g), axis=-1, keepdims=True)
    pg = 1.0 / jnp.sum(jnp.exp(gl - gmax), axis=-1, keepdims=True)
    eidx = lax.broadcasted_iota(jnp.int32, el.shape, 1)
    in_group = (eidx // (n_e // n_g)) == gsel
    neg_inf = jnp.float32(-jnp.inf)
    cand = jnp.where(in_group, el, neg_inf)
    v1 = jnp.max(cand, axis=-1, keepdims=True)
    i1 = jnp.min(jnp.where(cand == v1, eidx, n_e), axis=-1, keepdims=True)
    cand2 = jnp.where(eidx == i1, neg_inf, cand)
    v2 = jnp.max(cand2, axis=-1, keepdims=True)
    i2 = jnp.min(jnp.where(cand2 == v2, eidx, n_e), axis=-1, keepdims=True)
    r = jnp.exp(v2 - v1)
    w1 = pg / (1.0 + r)
    w2 = pg * r / (1.0 + r)
    return jnp.where(eidx == i1, w1, 0.0) + jnp.where(eidx == i2, w2, 0.0)


def _moe_kernel(h_ref, g_ref, wrc_ref, wrh_ref, bgr_ref, ber_ref, wg_ref, wu_ref, wd_ref, fn_ref,
                o_ref, hn_ref, comb_ref, *, final_norm):
    e = pl.program_id(1)

    @pl.when(e == 0)
    def _():
        x = h_ref[...]
        hn = _rmsnorm(x, g_ref[...])
        h_hi, h_lo = _split_bf16(hn, 2)
        comb_ref[...] = _route(h_hi, h_lo, wrc_ref[...], wrh_ref[...], bgr_ref[...], ber_ref[...])
        hn_ref[...] = h_hi
        o_ref[...] = x

    hn = hn_ref[...]
    comb = comb_ref[...]
    eidx = lax.broadcasted_iota(jnp.int32, comb.shape, 1)
    comb_e = jnp.sum(jnp.where(eidx == e, comb, 0.0), axis=-1, keepdims=True)
    hg = jnp.dot(hn, wg_ref[0, 0], preferred_element_type=F32)
    hu = jnp.dot(hn, wu_ref[0, 0], preferred_element_type=F32)
    act = (hg * _sigmoid(hg)) * hu * comb_e
    o_ref[...] += jnp.dot(act.astype(BF16), wd_ref[0, 0], preferred_element_type=F32)

    if final_norm:
        @pl.when(e == pl.num_programs(1) - 1)
        def _():
            o_ref[...] = _rmsnorm(o_ref[...], fn_ref[...])


def _moe_layer(h, layer, g, wgr, bgr, wer, ber, wg, wu, wd, fn=None):
    T, D = h.shape
    _, n_e, _, F = wg.shape
    n_g = wgr.shape[-1]
    tm = min(MOE_TOKEN_TILE, T)
    assert T % tm == 0
    final_norm = fn is not None
    if fn is None:
        fn = jnp.ones((D,), F32)
    wr_hi, wr_lo = _split_bf16(jnp.concatenate([wer, wgr], axis=1), 2)
    wr_cat = jnp.concatenate([wr_hi, wr_lo], axis=1)
    const2 = lambda i, e: (0, 0)
    full = lambda a: pl.BlockSpec(a.shape, const2)
    small = [g.reshape(1, D), wr_cat, wr_hi, bgr.reshape(1, n_g), ber.reshape(1, n_e)]
    return pl.pallas_call(
        functools.partial(_moe_kernel, final_norm=final_norm),
        grid=(T // tm, n_e),
        in_specs=[pl.BlockSpec((tm, D), lambda i, e: (i, 0))] + [full(a) for a in small] + [
            pl.BlockSpec((1, 1, D, F), lambda i, e: (layer, e, 0, 0)),
            pl.BlockSpec((1, 1, D, F), lambda i, e: (layer, e, 0, 0)),
            pl.BlockSpec((1, 1, F, D), lambda i, e: (layer, e, 0, 0)),
            pl.BlockSpec((1, D), const2),
        ],
        out_specs=pl.BlockSpec((tm, D), lambda i, e: (i, 0)),
        out_shape=jax.ShapeDtypeStruct((T, D), F32),
        scratch_shapes=[pltpu.VMEM((tm, D), BF16), pltpu.VMEM((tm, n_e), F32)],
        compiler_params=pltpu.CompilerParams(
            dimension_semantics=("arbitrary", "arbitrary"), vmem_limit_bytes=VMEM_LIMIT_BYTES),
        name="moe_final" if final_norm else "moe",
    )(h, *small, wg, wu, wd, fn.reshape(1, D))


def _mlstm_kernel(h_ref, g_ref, win_ref, wgate_ref, wgate_t_ref, convw_ref, convb_ref,
                  bi_ref, bf_ref, bi_t_ref, bf_t_ref, hnorm_ref, wout_ref, o_ref,
                  carry_ref, c_ref, n_ref, m_ref):
    @pl.when(pl.program_id(1) == 0)
    def _():
        carry_ref[...] = jnp.zeros_like(carry_ref)
        c_ref[...] = jnp.zeros_like(c_ref)
        n_ref[...] = jnp.zeros_like(n_ref)
        m_ref[...] = jnp.zeros_like(m_ref)

    nb, L, D = h_ref.shape
    H = MLSTM_HEADS
    qk_w = convw_ref.shape[1] // 2
    v_w = wout_ref.shape[0]
    x = h_ref[...].reshape(nb * L, D)
    hn = _rmsnorm(x, g_ref[...])
    hb = hn.astype(BF16)
    qk_pre = jnp.dot(hb, win_ref[:, :2 * qk_w], preferred_element_type=F32)
    v = jnp.dot(hb, win_ref[:, 2 * qk_w:2 * qk_w + v_w], preferred_element_type=F32)
    o_pre = jnp.dot(hb, win_ref[:, 2 * qk_w + v_w:2 * qk_w + 2 * v_w], preferred_element_type=F32)
    gates = jnp.dot(hb, wgate_ref[...], preferred_element_type=F32)
    gates = gates[:, :2 * H] + gates[:, 2 * H:]
    gates_t = lax.dot_general(wgate_t_ref[...], hb, (((1,), (1,)), ((), ())),
                              preferred_element_type=F32)
    gates_t = gates_t[:2 * H, :] + gates_t[2 * H:, :]
    outs = []
    for bb in range(nb):
        rows = slice(bb * L, (bb + 1) * L)
        outs.append(_mlstm_chunk(bb, qk_pre[rows], v[rows], gates[rows], gates_t[:, rows], convw_ref, convb_ref,
                                 bi_ref, bf_ref, bi_t_ref, bf_t_ref, carry_ref, c_ref, n_ref, m_ref))
    hcat = jnp.concatenate(outs, axis=0) * hnorm_ref[...] * _sigmoid(o_pre)
    out = x + jnp.dot(hcat.astype(BF16), wout_ref[...], preferred_element_type=F32)
    o_ref[...] = out.reshape(nb, L, D)


def _mlstm_chunk(bb, qk_pre, v, gates, gates_t, convw_ref, convb_ref, bi_ref, bf_ref, bi_t_ref, bf_t_ref,
                 carry_ref, c_ref, n_ref, m_ref):
    L = qk_pre.shape[0]
    H = MLSTM_HEADS
    qk_w = convw_ref.shape[1] // 2
    dqk = qk_w // H
    dv = v.shape[1] // H
    assert 2 * dqk == LANES and dv == LANES

    ext = jnp.concatenate([carry_ref[bb], qk_pre], axis=0)
    carry_ref[bb] = qk_pre[L - CONV_CARRY:, :]
    conv = ext * convw_ref[CONV_K - 1:CONV_K, :]
    for d in range(1, CONV_K):
        conv = conv + pltpu.roll(ext, d, 0) * convw_ref[CONV_K - 1 - d:CONV_K - d, :]
    qk = conv[CONV_CARRY:, :] + convb_ref[...]
    qk = qk * _sigmoid(qk)
    q = qk[:, :qk_w]
    k = qk[:, qk_w:] * (dqk ** -0.5)
    qb = q.astype(BF16)
    kb = k.astype(BF16)
    vb = v.astype(BF16)

    li = gates[:, :H] + bi_ref[...]
    lf = _log_sigmoid(gates[:, H:] + bf_ref[...])
    li_t = gates_t[:H, :] + bi_t_ref[...]
    lf_t = _log_sigmoid(gates_t[H:, :] + bf_t_ref[...])
    row = lax.broadcasted_iota(jnp.int32, (L, L), 0)
    col = lax.broadcasted_iota(jnp.int32, (L, L), 1)
    causal = row >= col
    tri = causal.astype(BF16)
    tri_t = (row <= col).astype(BF16)
    b = sum(jnp.dot(tri, part, preferred_element_type=F32) for part in _split_bf16(lf, 3))
    b_t = sum(jnp.dot(part, tri_t, preferred_element_type=F32) for part in _split_bf16(lf_t, 3))

    m_prev = m_ref[bb]
    inter = b + m_prev
    b_last = b[L - 1:L, :]
    gsum = b_last - b + li
    m_new = jnp.maximum(b_last + m_prev, jnp.max(gsum, axis=0, keepdims=True))
    w_s = jnp.exp(gsum - m_new)
    decay = jnp.exp(b_last + m_prev - m_new)
    m_ref[bb] = m_new

    lane = lax.broadcasted_iota(jnp.int32, (1, LANES), 1)
    first = lane < dqk
    prow = lax.broadcasted_iota(jnp.int32, (LANES, 1), 0) < dqk
    on_diag = prow == (lax.broadcasted_iota(jnp.int32, (1, 2 * dv), 1) < dv)
    neg_inf = jnp.float32(-jnp.inf)
    zeros_v = jnp.zeros((L, dv), BF16)
    outs = []
    for p in range(H // 2):
        sl = slice(p * LANES, (p + 1) * LANES)
        q2, k2, q2f, k2f = qb[:, sl], kb[:, sl], q[:, sl], k[:, sl]
        heads = (2 * p, 2 * p + 1)
        c_pair = c_ref[bb, p]
        n_pair = n_ref[bb, :, sl]
        k_stack = jnp.concatenate([jnp.where(first, k2, jnp.zeros_like(k2)),
                                   jnp.where(first, jnp.zeros_like(k2), k2)], axis=0)
        s_pair = lax.dot_general(q2, k_stack, (((1,), (1,)), ((), ())), preferred_element_type=F32)
        inter_pair = jnp.dot(q2, c_pair.astype(BF16), preferred_element_type=F32)
        scs, m_ts, s_inters = [], [], []
        for half, hh in enumerate(heads):
            dm = jnp.where(causal, b[:, hh:hh + 1] - (b_t[hh:hh + 1, :] - li_t[hh:hh + 1, :]), neg_inf)
            m_t = jnp.maximum(inter[:, hh:hh + 1], jnp.max(dm, axis=-1, keepdims=True))
            scs.append(s_pair[:, half * L:(half + 1) * L] * jnp.exp(dm - m_t))
            m_ts.append(m_t)
            s_inters.append(jnp.exp(inter[:, hh:hh + 1] - m_t))
        va, vb2 = (vb[:, hh * dv:(hh + 1) * dv] for hh in heads)
        v_diag = jnp.concatenate([jnp.concatenate([va, zeros_v], axis=1),
                                  jnp.concatenate([zeros_v, vb2], axis=1)], axis=0)
        num_pair = jnp.dot(jnp.concatenate(scs, axis=1).astype(BF16), v_diag, preferred_element_type=F32)
        for half, hh in enumerate(heads):
            own = first if half == 0 else jnp.logical_not(first)
            hs = slice(half * dv, (half + 1) * dv)
            num = num_pair[:, hs] + s_inters[half] * inter_pair[:, hs]
            qn = jnp.sum(jnp.where(own, q2f * n_pair, 0.0), axis=-1, keepdims=True)
            den = jnp.sum(scs[half], axis=-1, keepdims=True) + s_inters[half] * qn
            ho = num / jnp.maximum(jnp.abs(den), jnp.exp(-m_ts[half]))
            outs.append(ho * lax.rsqrt(jnp.mean(ho * ho, axis=-1, keepdims=True) + EPS))
        wsv = jnp.concatenate([(w_s[:, hh:hh + 1] * v[:, hh * dv:(hh + 1) * dv]).astype(BF16) for hh in heads],
                              axis=1)
        upd = lax.dot_general(k2, wsv, (((0,), (0,)), ((), ())), preferred_element_type=F32)
        d0, d1 = decay[:, 2 * p:2 * p + 1], decay[:, 2 * p + 1:2 * p + 2]
        c_ref[bb, p] = jnp.where(prow, d0, d1) * c_pair + jnp.where(on_diag, upd, 0.0)
        ws_pair = jnp.where(first, w_s[:, 2 * p:2 * p + 1], w_s[:, 2 * p + 1:2 * p + 2])
        n_ref[bb, :, sl] = jnp.where(first, d0, d1) * n_pair + jnp.sum(k2f * ws_pair, axis=0, keepdims=True)

    return jnp.concatenate(outs, axis=-1)


def _mlstm_layer(h, g, w_in, conv_w, conv_b, b_i, b_f, head_norm, w_out):
    B, S, D = h.shape
    H = MLSTM_HEADS
    v_w = w_out.shape[0]
    qk2 = conv_w.shape[1]
    assert w_in.shape[1] == qk2 + 2 * v_w + 2 * H
    L = min(MLSTM_CHUNK, S)
    nb = MLSTM_BATCH_PER_STEP
    assert S % L == 0 and B % nb == 0
    w_gate = jnp.concatenate(_split_bf16(w_in[:, qk2 + 2 * v_w:], 2), axis=1)
    const2 = lambda b, j: (0, 0)
    full = lambda a: pl.BlockSpec(a.shape, const2)
    operands = [g.reshape(1, D), w_in.astype(BF16), w_gate, w_gate.T, conv_w, conv_b.reshape(1, qk2),
                b_i.reshape(1, H), b_f.reshape(1, H), b_i.reshape(H, 1), b_f.reshape(H, 1),
                head_norm.reshape(1, v_w), w_out.astype(BF16)]
    return pl.pallas_call(
        _mlstm_kernel,
        grid=(B // nb, S // L),
        in_specs=[pl.BlockSpec((nb, L, D), lambda b, j: (b, j, 0))] + [full(a) for a in operands],
        out_specs=pl.BlockSpec((nb, L, D), lambda b, j: (b, j, 0)),
        out_shape=jax.ShapeDtypeStruct((B, S, D), F32),
        scratch_shapes=[
            pltpu.VMEM((nb, CONV_CARRY, qk2), F32),
            pltpu.VMEM((nb, H // 2, LANES, 2 * v_w // H), F32),
            pltpu.VMEM((nb, 1, qk2 // 2), F32),
            pltpu.VMEM((nb, 1, H), F32),
        ],
        compiler_params=pltpu.CompilerParams(
            dimension_semantics=("arbitrary", "arbitrary"), vmem_limit_bytes=VMEM_LIMIT_BYTES),
        name="mlstm_mixer",
    )(h, *operands)


def kernel(x, mix_norm, pool_w, pool_scale, w_in, conv_w, conv_b, gate_bias_i, gate_bias_f, head_norm,
           w_out, ffn_norm, w_group_router, b_group_router, w_expert_router, b_expert_router, w_gate,
           w_up, w_down, final_norm):
    B, S, D = x.shape
    depth = mix_norm.shape[0]
    n_mixers = 2
    wg_b, wu_b, wd_b = w_gate.astype(BF16), w_up.astype(BF16), w_down.astype(BF16)
    h = x
    for i in range(depth):
        j = i // n_mixers
        if i % n_mixers == 0:
            h = _pool_layer(h, mix_norm[i], pool_w[j], pool_scale[j])
        else:
            h = _mlstm_layer(h, mix_norm[i], w_in[j], conv_w[j], conv_b[j], gate_bias_i[j],
                             gate_bias_f[j], head_norm[j], w_out[j])
        fn = final_norm if i == depth - 1 else None
        h = _moe_layer(h.reshape(B * S, D), i, ffn_norm[i], w_group_router[i], b_group_router[i],
                       w_expert_router[i], b_expert_router[i], wg_b, wu_b, wd_b, fn).reshape(B, S, D)
    return h
```

```python
import functools

import jax
import jax.numpy as jnp
from jax import lax
from jax.experimental import pallas as pl
from jax.experimental.pallas import tpu as pltpu

EPS = 1e-6
POOL_WINDOWS = (2, 4, 8, 16)
MLSTM_HEADS = 8
CONV_K = 4
N_GROUPS = 4
EXPERTS_PER_GROUP = 4
N_EXPERTS = N_GROUPS * EXPERTS_PER_GROUP

SUBLANES = 8
LANES = 128
VMEM_LIMIT_BYTES = 56 * 1024 * 1024

POOL_SEQ_TILE = 512
POOL_HALO = 16
MOE_TOKEN_TILE = 1024
MOE_EXPERTS_PER_STEP = 4
ROUTER_LO_LANE = 32
MLSTM_CHUNK = 256
MLSTM_BATCH_PER_STEP = 2
CONV_CARRY = SUBLANES

F32 = jnp.float32
BF16 = jnp.bfloat16
HIGHEST = lax.Precision.HIGHEST


def _rmsnorm(v, g):
    return (v * lax.rsqrt(jnp.mean(v * v, axis=-1, keepdims=True) + EPS)) * g


def _sigmoid(v):
    return 1.0 / (1.0 + jnp.exp(-v))


def _log_sigmoid(v):
    return jnp.minimum(v, 0.0) - jnp.log1p(jnp.exp(-jnp.abs(v)))


def _pool_kernel(x_ref, halo_ref, g_ref, w_ref, scale_ref, o_ref):
    j = pl.program_id(1)
    ts = x_ref.shape[1]
    gc = w_ref.shape[1]
    x = x_ref[0]
    g = g_ref[...]
    hn = _rmsnorm(x, g)
    hh = _rmsnorm(halo_ref[0], g) * (j > 0).astype(F32)
    s = jnp.concatenate([hh, hn], axis=0)
    sums = []
    shift = 1
    for _ in POOL_WINDOWS:
        s = s + pltpu.roll(s, shift, 0)
        sums.append(s[POOL_HALO:, :gc])
        s = s[:, gc:]
        shift *= 2
    t = (j * ts + lax.broadcasted_iota(jnp.int32, (ts, 1), 0) + 1).astype(F32)
    ys = []
    for k, w in enumerate(POOL_WINDOWS):
        count = jnp.minimum(t, float(w))
        pooled = sums[k] / count - hn[:, k * gc:(k + 1) * gc]
        ys.append(jnp.dot(pooled.astype(BF16), w_ref[k], preferred_element_type=F32))
    y = jnp.concatenate(ys, axis=-1)
    o_ref[0] = x + y * scale_ref[...]


def _pool_layer(x, g, w, scale):
    B, S, D = x.shape
    ts = min(POOL_SEQ_TILE, S)
    assert S % ts == 0 and ts % POOL_HALO == 0
    n_groups, gc, _ = w.shape
    assert n_groups == len(POOL_WINDOWS) and n_groups * gc == D
    halo_blocks = ts // POOL_HALO
    return pl.pallas_call(
        _pool_kernel,
        grid=(B, S // ts),
        in_specs=[
            pl.BlockSpec((1, ts, D), lambda b, j: (b, j, 0)),
            pl.BlockSpec((1, POOL_HALO, D), lambda b, j: (b, jnp.maximum(j * halo_blocks - 1, 0), 0)),
            pl.BlockSpec((1, D), lambda b, j: (0, 0)),
            pl.BlockSpec((n_groups, gc, gc), lambda b, j: (0, 0, 0)),
            pl.BlockSpec((1, D), lambda b, j: (0, 0)),
        ],
        out_specs=pl.BlockSpec((1, ts, D), lambda b, j: (b, j, 0)),
        out_shape=jax.ShapeDtypeStruct((B, S, D), F32),
        compiler_params=pltpu.CompilerParams(
            dimension_semantics=("arbitrary", "arbitrary"), vmem_limit_bytes=VMEM_LIMIT_BYTES),
        name="pool_mixer",
    )(x, x, g.reshape(1, D), w.astype(BF16), scale.reshape(1, D))


def _split_bf16(a, parts):
    out = []
    for _ in range(parts - 1):
        hi = a.astype(BF16)
        out.append(hi)
        a = a - hi.astype(F32)
    out.append(a.astype(BF16))
    return out


def _route_t(logits_t, bgr, ber):
    n_e, n_g = ber.shape[0], bgr.shape[0]
    el = logits_t[:n_e, :] + ber
    gl = logits_t[n_e:n_e + n_g, :] + bgr
    gidx = lax.broadcasted_iota(jnp.int32, gl.shape, 0)
    gmax = jnp.max(gl, axis=0, keepdims=True)
    gsel = jnp.min(jnp.where(gl == gmax, gidx, n_g), axis=0, keepdims=True)
    pg = 1.0 / jnp.sum(jnp.exp(gl - gmax), axis=0, keepdims=True)
    eidx = lax.broadcasted_iota(jnp.int32, el.shape, 0)
    in_group = (eidx // (n_e // n_g)) == gsel
    neg_inf = jnp.float32(-jnp.inf)
    cand = jnp.where(in_group, el, neg_inf)
    v1 = jnp.max(cand, axis=0, keepdims=True)
    i1 = jnp.min(jnp.where(cand == v1, eidx, n_e), axis=0, keepdims=True)
    cand2 = jnp.where(eidx == i1, neg_inf, cand)
    v2 = jnp.max(cand2, axis=0, keepdims=True)
    i2 = jnp.min(jnp.where(cand2 == v2, eidx, n_e), axis=0, keepdims=True)
    r = jnp.exp(v2 - v1)
    w1 = pg / (1.0 + r)
    w2 = pg * r / (1.0 + r)
    return jnp.where(eidx == i1, w1, 0.0) + jnp.where(eidx == i2, w2, 0.0)


def _moe_kernel(h_ref, g_ref, wrc_ref, wrh_ref, bgr_ref, ber_ref, wg_ref, wu_ref, wd_ref, fn_ref,
                o_ref, hn_ref, comb_ref, *, final_norm):
    step = pl.program_id(1)
    eps = wg_ref.shape[1]
    tm = h_ref.shape[0]

    @pl.when(step == 0)
    def _():
        x = h_ref[...]
        hn = _rmsnorm(x, g_ref[...])
        h_hi, h_lo = _split_bf16(hn, 2)
        logits = (jnp.dot(h_hi, wrc_ref[...], preferred_element_type=F32)
                  + jnp.dot(h_lo, wrh_ref[...], preferred_element_type=F32))
        logits_t = logits.T
        n_l = ROUTER_LO_LANE
        comb_t = _route_t(logits_t[:n_l, :] + logits_t[n_l:2 * n_l, :], bgr_ref[...], ber_ref[...])
        comb_t = jnp.concatenate([comb_t, jnp.zeros((LANES - comb_t.shape[0], tm), F32)], axis=0)
        comb_ref[...] = comb_t.T
        hn_ref[...] = h_hi
        o_ref[...] = x

    hn = hn_ref[...]
    comb = comb_ref[...]
    eidx = lax.broadcasted_iota(jnp.int32, comb.shape, 1)
    acts = []
    for jj in range(eps):
        comb_e = jnp.sum(jnp.where(eidx == step * eps + jj, comb, 0.0), axis=-1, keepdims=True)
        hg = jnp.dot(hn, wg_ref[0, jj], preferred_element_type=F32)
        hu = jnp.dot(hn, wu_ref[0, jj], preferred_element_type=F32)
        acts.append(((hg * _sigmoid(hg)) * hu * comb_e).astype(BF16))
    n_f, n_d = wd_ref.shape[2:]
    o_ref[...] += jnp.dot(jnp.concatenate(acts, axis=-1), wd_ref[0].reshape(eps * n_f, n_d),
                          preferred_element_type=F32)

    if final_norm:
        @pl.when(step == pl.num_programs(1) - 1)
        def _():
            o_ref[...] = _rmsnorm(o_ref[...], fn_ref[...])


def _moe_layer(h, layer, g, wgr, bgr, wer, ber, wg, wu, wd, fn=None):
    T, D = h.shape
    _, n_e, _, F = wg.shape
    n_g = wgr.shape[-1]
    tm = min(MOE_TOKEN_TILE, T)
    eps = MOE_EXPERTS_PER_STEP
    assert T % tm == 0 and n_e % eps == 0 and n_e + n_g <= ROUTER_LO_LANE
    final_norm = fn is not None
    if fn is None:
        fn = jnp.ones((D,), F32)
    wr_hi, wr_lo = _split_bf16(jnp.concatenate([wer, wgr], axis=1), 2)
    pad = lambda a, n: jnp.pad(a, ((0, 0), (0, n - a.shape[1])))
    wr_cat = jnp.concatenate([pad(wr_hi, ROUTER_LO_LANE), pad(wr_lo, LANES - ROUTER_LO_LANE)], axis=1)
    const2 = lambda i, e: (0, 0)
    full = lambda a: pl.BlockSpec(a.shape, const2)
    small = [g.reshape(1, D), wr_cat, pad(wr_hi, LANES), bgr.reshape(n_g, 1), ber.reshape(n_e, 1)]
    return pl.pallas_call(
        functools.partial(_moe_kernel, final_norm=final_norm),
        grid=(T // tm, n_e // eps),
        in_specs=[pl.BlockSpec((tm, D), lambda i, e: (i, 0))] + [full(a) for a in small] + [
            pl.BlockSpec((1, eps, D, F), lambda i, e: (layer, e, 0, 0)),
            pl.BlockSpec((1, eps, D, F), lambda i, e: (layer, e, 0, 0)),
            pl.BlockSpec((1, eps, F, D), lambda i, e: (layer, e, 0, 0)),
            pl.BlockSpec((1, D), const2),
        ],
        out_specs=pl.BlockSpec((tm, D), lambda i, e: (i, 0)),
        out_shape=jax.ShapeDtypeStruct((T, D), F32),
        scratch_shapes=[pltpu.VMEM((tm, D), BF16), pltpu.VMEM((tm, LANES), F32)],
        compiler_params=pltpu.CompilerParams(
            dimension_semantics=("arbitrary", "arbitrary"), vmem_limit_bytes=VMEM_LIMIT_BYTES),
        name="moe_final" if final_norm else "moe",
    )(h, *small, wg, wu, wd, fn.reshape(1, D))


def _mlstm_kernel(h_ref, g_ref, win_ref, wgate_ref, wgate_t_ref, convw_ref, convb_ref,
                  bi_ref, bf_ref, bi_t_ref, bf_t_ref, hnorm_ref, wout_ref, o_ref,
                  carry_ref, c_ref, n_ref, m_ref):
    @pl.when(pl.program_id(1) == 0)
    def _():
        carry_ref[...] = jnp.zeros_like(carry_ref)
        c_ref[...] = jnp.zeros_like(c_ref)
        n_ref[...] = jnp.zeros_like(n_ref)
        m_ref[...] = jnp.zeros_like(m_ref)

    nb, L, D = h_ref.shape
    H = MLSTM_HEADS
    qk_w = convw_ref.shape[1] // 2
    v_w = wout_ref.shape[0]
    x = h_ref[...].reshape(nb * L, D)
    hn = _rmsnorm(x, g_ref[...])
    hb = hn.astype(BF16)
    qk_pre = jnp.dot(hb, win_ref[:, :2 * qk_w], preferred_element_type=F32)
    v = jnp.dot(hb, win_ref[:, 2 * qk_w:2 * qk_w + v_w], preferred_element_type=F32)
    o_pre = jnp.dot(hb, win_ref[:, 2 * qk_w + v_w:2 * qk_w + 2 * v_w], preferred_element_type=F32)
    gates = jnp.dot(hb, wgate_ref[...], preferred_element_type=F32)
    gates = gates[:, :2 * H] + gates[:, 2 * H:]
    gates_t = lax.dot_general(wgate_t_ref[...], hb, (((1,), (1,)), ((), ())),
                              preferred_element_type=F32)
    gates_t = gates_t[:2 * H, :] + gates_t[2 * H:, :]
    outs = []
    for bb in range(nb):
        rows = slice(bb * L, (bb + 1) * L)
        outs.append(_mlstm_chunk(bb, qk_pre[rows], v[rows], gates[rows], gates_t[:, rows], convw_ref, convb_ref,
                                 bi_ref, bf_ref, bi_t_ref, bf_t_ref, carry_ref, c_ref, n_ref, m_ref))
    hcat = jnp.concatenate(outs, axis=0) * hnorm_ref[...] * _sigmoid(o_pre)
    out = x + jnp.dot(hcat.astype(BF16), wout_ref[...], preferred_element_type=F32)
    o_ref[...] = out.reshape(nb, L, D)


def _mlstm_chunk(bb, qk_pre, v, gates, gates_t, convw_ref, convb_ref, bi_ref, bf_ref, bi_t_ref, bf_t_ref,
                 carry_ref, c_ref, n_ref, m_ref):
    L = qk_pre.shape[0]
    H = MLSTM_HEADS
    qk_w = convw_ref.shape[1] // 2
    dqk = qk_w // H
    dv = v.shape[1] // H
    assert 2 * dqk == LANES and dv == LANES

    ext = jnp.concatenate([carry_ref[bb], qk_pre], axis=0)
    carry_ref[bb] = qk_pre[L - CONV_CARRY:, :]
    conv = ext * convw_ref[CONV_K - 1:CONV_K, :]
    for d in range(1, CONV_K):
        conv = conv + pltpu.roll(ext, d, 0) * convw_ref[CONV_K - 1 - d:CONV_K - d, :]
    qk = conv[CONV_CARRY:, :] + convb_ref[...]
    qk = qk * _sigmoid(qk)
    q = qk[:, :qk_w]
    k = qk[:, qk_w:] * (dqk ** -0.5)
    qb = q.astype(BF16)
    kb = k.astype(BF16)
    vb = v.astype(BF16)

    li = gates[:, :H] + bi_ref[...]
    lf = _log_sigmoid(gates[:, H:] + bf_ref[...])
    li_t = gates_t[:H, :] + bi_t_ref[...]
    lf_t = _log_sigmoid(gates_t[H:, :] + bf_t_ref[...])
    row = lax.broadcasted_iota(jnp.int32, (L, L), 0)
    col = lax.broadcasted_iota(jnp.int32, (L, L), 1)
    causal = row >= col
    tri = causal.astype(BF16)
    tri_t = (row <= col).astype(BF16)
    b = sum(jnp.dot(tri, part, preferred_element_type=F32) for part in _split_bf16(lf, 3))
    b_t = sum(jnp.dot(part, tri_t, preferred_element_type=F32) for part in _split_bf16(lf_t, 3))

    m_prev = m_ref[bb]
    inter = b + m_prev
    b_last = b[L - 1:L, :]
    gsum = b_last - b + li
    m_new = jnp.maximum(b_last + m_prev, jnp.max(gsum, axis=0, keepdims=True))
    w_s = jnp.exp(gsum - m_new)
    decay = jnp.exp(b_last + m_prev - m_new)
    m_ref[bb] = m_new

    lane = lax.broadcasted_iota(jnp.int32, (1, LANES), 1)
    first = lane < dqk
    prow = lax.broadcasted_iota(jnp.int32, (LANES, 1), 0) < dqk
    on_diag = prow == (lax.broadcasted_iota(jnp.int32, (1, 2 * dv), 1) < dv)
    neg_inf = jnp.float32(-jnp.inf)
    zeros_v = jnp.zeros((L, dv), BF16)
    outs = []
    for p in range(H // 2):
        sl = slice(p * LANES, (p + 1) * LANES)
        q2, k2, q2f, k2f = qb[:, sl], kb[:, sl], q[:, sl], k[:, sl]
        heads = (2 * p, 2 * p + 1)
        c_pair = c_ref[bb, p]
        n_pair = n_ref[bb, :, sl]
        k_stack = jnp.concatenate([jnp.where(first, k2, jnp.zeros_like(k2)),
                                   jnp.where(first, jnp.zeros_like(k2), k2)], axis=0)
        s_pair = lax.dot_general(q2, k_stack, (((1,), (1,)), ((), ())), preferred_element_type=F32)
        inter_pair = jnp.dot(q2, c_pair.astype(BF16), preferred_element_type=F32)
        scs, m_ts, s_inters = [], [], []
        for half, hh in enumerate(heads):
            dm = jnp.where(causal, b[:, hh:hh + 1] - (b_t[hh:hh + 1, :] - li_t[hh:hh + 1, :]), neg_inf)
            m_t = jnp.maximum(inter[:, hh:hh + 1], jnp.max(dm, axis=-1, keepdims=True))
            scs.append(s_pair[:, half * L:(half + 1) * L] * jnp.exp(dm - m_t))
            m_ts.append(m_t)
            s_inters.append(jnp.exp(inter[:, hh:hh + 1] - m_t))
        va, vb2 = (vb[:, hh * dv:(hh + 1) * dv] for hh in heads)
        v_diag = jnp.concatenate([jnp.concatenate([va, zeros_v], axis=1),
                                  jnp.concatenate([zeros_v, vb2], axis=1)], axis=0)
        num_pair = jnp.dot(jnp.concatenate(scs, axis=1).astype(BF16), v_diag, preferred_element_type=F32)
        for half, hh in enumerate(heads):
            own = first if half == 0 else jnp.logical_not(first)
            hs = slice(half * dv, (half + 1) * dv)
            num = num_pair[:, hs] + s_inters[half] * inter_pair[:, hs]
            qn = jnp.sum(jnp.where(own, q2f * n_pair, 0.0), axis=-1, keepdims=True)
            den = jnp.sum(scs[half], axis=-1, keepdims=True) + s_inters[half] * qn
            ho = num / jnp.maximum(jnp.abs(den), jnp.exp(-m_ts[half]))
            outs.append(ho * lax.rsqrt(jnp.mean(ho * ho, axis=-1, keepdims=True) + EPS))
        wsv = jnp.concatenate([(w_s[:, hh:hh + 1] * v[:, hh * dv:(hh + 1) * dv]).astype(BF16) for hh in heads],
                              axis=1)
        upd = lax.dot_general(k2, wsv, (((0,), (0,)), ((), ())), preferred_element_type=F32)
        d0, d1 = decay[:, 2 * p:2 * p + 1], decay[:, 2 * p + 1:2 * p + 2]
        c_ref[bb, p] = jnp.where(prow, d0, d1) * c_pair + jnp.where(on_diag, upd, 0.0)
        ws_pair = jnp.where(first, w_s[:, 2 * p:2 * p + 1], w_s[:, 2 * p + 1:2 * p + 2])
        n_ref[bb, :, sl] = jnp.where(first, d0, d1) * n_pair + jnp.sum(k2f * ws_pair, axis=0, keepdims=True)

    return jnp.concatenate(outs, axis=-1)


def _mlstm_layer(h, g, w_in, conv_w, conv_b, b_i, b_f, head_norm, w_out):
    B, S, D = h.shape
    H = MLSTM_HEADS
    v_w = w_out.shape[0]
    qk2 = conv_w.shape[1]
    assert w_in.shape[1] == qk2 + 2 * v_w + 2 * H
    L = min(MLSTM_CHUNK, S)
    nb = MLSTM_BATCH_PER_STEP
    assert S % L == 0 and B % nb == 0
    w_gate = jnp.concatenate(_split_bf16(w_in[:, qk2 + 2 * v_w:], 2), axis=1)
    const2 = lambda b, j: (0, 0)
    full = lambda a: pl.BlockSpec(a.shape, const2)
    operands = [g.reshape(1, D), w_in.astype(BF16), w_gate, w_gate.T, conv_w, conv_b.reshape(1, qk2),
                b_i.reshape(1, H), b_f.reshape(1, H), b_i.reshape(H, 1), b_f.reshape(H, 1),
                head_norm.reshape(1, v_w), w_out.astype(BF16)]
    return pl.pallas_call(
        _mlstm_kernel,
        grid=(B // nb, S // L),
        in_specs=[pl.BlockSpec((nb, L, D), lambda b, j: (b, j, 0))] + [full(a) for a in operands],
        out_specs=pl.BlockSpec((nb, L, D), lambda b, j: (b, j, 0)),
        out_shape=jax.ShapeDtypeStruct((B, S, D), F32),
        scratch_shapes=[
            pltpu.VMEM((nb, CONV_CARRY, qk2), F32),
            pltpu.VMEM((nb, H // 2, LANES, 2 * v_w // H), F32),
            pltpu.VMEM((nb, 1, qk2 // 2), F32),
            pltpu.VMEM((nb, 1, H), F32),
        ],
        compiler_params=pltpu.CompilerParams(
            dimension_semantics=("arbitrary", "arbitrary"), vmem_limit_bytes=VMEM_LIMIT_BYTES),
        name="mlstm_mixer",
    )(h, *operands)


def kernel(x, mix_norm, pool_w, pool_scale, w_in, conv_w, conv_b, gate_bias_i, gate_bias_f, head_norm,
           w_out, ffn_norm, w_group_router, b_group_router, w_expert_router, b_expert_router, w_gate,
           w_up, w_down, final_norm):
    B, S, D = x.shape
    depth = mix_norm.shape[0]
    n_mixers = 2
    wg_b, wu_b, wd_b = w_gate.astype(BF16), w_up.astype(BF16), w_down.astype(BF16)
    h = x
    for i in range(depth):
        j = i // n_mixers
        if i % n_mixers == 0:
            h = _pool_layer(h, mix_norm[i], pool_w[j], pool_scale[j])
        else:
            h = _mlstm_layer(h, mix_norm[i], w_in[j], conv_w[j], conv_b[j], gate_bias_i[j],
                             gate_bias_f[j], head_norm[j], w_out[j])
        fn = final_norm if i == depth - 1 else None
        h = _moe_layer(h.reshape(B * S, D), i, ffn_norm[i], w_group_router[i], b_group_router[i],
                       w_expert_router[i], b_expert_router[i], wg_b, wu_b, wd_b, fn).reshape(B, S, D)
    return h
```

```python
import functools

import jax
import jax.numpy as jnp
from jax import lax
from jax.experimental import pallas as pl
from jax.experimental.pallas import tpu as pltpu

EPS = 1e-6
POOL_WINDOWS = (2, 4, 8, 16)
MLSTM_HEADS = 8
CONV_K = 4

SUBLANES = 8
LANES = 128
VMEM_LIMIT_BYTES = 56 * 1024 * 1024

POOL_SEQ_TILE = 512
POOL_HALO = 16
MOE_TOKEN_TILE = 1024
MOE_EXPERTS_PER_STEP = 4
ROUTER_LO_LANE = 32
MLSTM_CHUNK = 256
MLSTM_BATCH_PER_STEP = 2
CONV_CARRY = SUBLANES
STATE_EXTRA_ROWS = 16

F32 = jnp.float32
BF16 = jnp.bfloat16


def _rmsnorm(v, g):
    return (v * lax.rsqrt(jnp.mean(v * v, axis=-1, keepdims=True) + EPS)) * g


def _sigmoid(v):
    return 1.0 / (1.0 + jnp.exp(-v))


def _log_sigmoid(v):
    return jnp.minimum(v, 0.0) - jnp.log1p(jnp.exp(-jnp.abs(v)))


def _split_bf16(a, parts):
    out = []
    for _ in range(parts - 1):
        hi = a.astype(BF16)
        out.append(hi)
        a = a - hi.astype(F32)
    out.append(a.astype(BF16))
    return out


def _pool_kernel(x_ref, halo_ref, g_ref, w_ref, scale_ref, o_ref):
    j = pl.program_id(1)
    ts = x_ref.shape[1]
    gc = w_ref.shape[1]
    x = x_ref[0]
    g = g_ref[...]
    hn = _rmsnorm(x, g)
    hh = _rmsnorm(halo_ref[0], g) * (j > 0).astype(F32)
    s = jnp.concatenate([hh, hn], axis=0)
    sums = []
    shift = 1
    for _ in POOL_WINDOWS:
        s = s + pltpu.roll(s, shift, 0)
        sums.append(s[POOL_HALO:, :gc])
        s = s[:, gc:]
        shift *= 2
    t = (j * ts + lax.broadcasted_iota(jnp.int32, (ts, 1), 0) + 1).astype(F32)
    ys = []
    for k, w in enumerate(POOL_WINDOWS):
        count = jnp.minimum(t, float(w))
        pooled = sums[k] / count - hn[:, k * gc:(k + 1) * gc]
        ys.append(jnp.dot(pooled.astype(BF16), w_ref[k], preferred_element_type=F32))
    y = jnp.concatenate(ys, axis=-1)
    o_ref[0] = x + y * scale_ref[...]


def _pool_layer(x, g, w, scale):
    B, S, D = x.shape
    ts = min(POOL_SEQ_TILE, S)
    assert S % ts == 0 and ts % POOL_HALO == 0
    n_groups, gc, _ = w.shape
    assert n_groups == len(POOL_WINDOWS) and n_groups * gc == D
    halo_blocks = ts // POOL_HALO
    return pl.pallas_call(
        _pool_kernel,
        grid=(B, S // ts),
        in_specs=[
            pl.BlockSpec((1, ts, D), lambda b, j: (b, j, 0)),
            pl.BlockSpec((1, POOL_HALO, D), lambda b, j: (b, jnp.maximum(j * halo_blocks - 1, 0), 0)),
            pl.BlockSpec((1, D), lambda b, j: (0, 0)),
            pl.BlockSpec((n_groups, gc, gc), lambda b, j: (0, 0, 0)),
            pl.BlockSpec((1, D), lambda b, j: (0, 0)),
        ],
        out_specs=pl.BlockSpec((1, ts, D), lambda b, j: (b, j, 0)),
        out_shape=jax.ShapeDtypeStruct((B, S, D), F32),
        compiler_params=pltpu.CompilerParams(
            dimension_semantics=("arbitrary", "arbitrary"), vmem_limit_bytes=VMEM_LIMIT_BYTES),
        name="pool_mixer",
    )(x, x, g.reshape(1, D), w.astype(BF16), scale.reshape(1, D))


def _route_t(logits_t, bgr, ber):
    n_e, n_g = ber.shape[0], bgr.shape[0]
    el = logits_t[:n_e, :] + ber
    gl = logits_t[n_e:n_e + n_g, :] + bgr
    gidx = lax.broadcasted_iota(jnp.int32, gl.shape, 0)
    gmax = jnp.max(gl, axis=0, keepdims=True)
    gsel = jnp.min(jnp.where(gl == gmax, gidx, n_g), axis=0, keepdims=True)
    pg = 1.0 / jnp.sum(jnp.exp(gl - gmax), axis=0, keepdims=True)
    eidx = lax.broadcasted_iota(jnp.int32, el.shape, 0)
    in_group = (eidx // (n_e // n_g)) == gsel
    neg_inf = jnp.float32(-jnp.inf)
    cand = jnp.where(in_group, el, neg_inf)
    v1 = jnp.max(cand, axis=0, keepdims=True)
    i1 = jnp.min(jnp.where(cand == v1, eidx, n_e), axis=0, keepdims=True)
    cand2 = jnp.where(eidx == i1, neg_inf, cand)
    v2 = jnp.max(cand2, axis=0, keepdims=True)
    i2 = jnp.min(jnp.where(cand2 == v2, eidx, n_e), axis=0, keepdims=True)
    r = jnp.exp(v2 - v1)
    w1 = pg / (1.0 + r)
    w2 = pg * r / (1.0 + r)
    return jnp.where(eidx == i1, w1, 0.0) + jnp.where(eidx == i2, w2, 0.0)


def _moe_kernel(h_ref, g_ref, wrc_ref, wrh_ref, bgr_ref, ber_ref, wg_ref, wu_ref, wd_ref, fn_ref,
                o_ref, hn_ref, comb_ref, *, final_norm):
    step = pl.program_id(1)
    eps = wg_ref.shape[1]
    tm = h_ref.shape[0]

    @pl.when(step == 0)
    def _():
        x = h_ref[...]
        hn = _rmsnorm(x, g_ref[...])
        h_hi, h_lo = _split_bf16(hn, 2)
        logits = (jnp.dot(h_hi, wrc_ref[...], preferred_element_type=F32)
                  + jnp.dot(h_lo, wrh_ref[...], preferred_element_type=F32))
        logits_t = logits.T
        n_l = ROUTER_LO_LANE
        comb_t = _route_t(logits_t[:n_l, :] + logits_t[n_l:2 * n_l, :], bgr_ref[...], ber_ref[...])
        comb_t = jnp.concatenate([comb_t, jnp.zeros((LANES - comb_t.shape[0], tm), F32)], axis=0)
        comb_ref[...] = comb_t.T
        hn_ref[...] = h_hi
        o_ref[...] = x

    hn = hn_ref[...]
    comb = comb_ref[...]
    eidx = lax.broadcasted_iota(jnp.int32, comb.shape, 1)
    acts = []
    for jj in range(eps):
        comb_e = jnp.sum(jnp.where(eidx == step * eps + jj, comb, 0.0), axis=-1, keepdims=True)
        hg = jnp.dot(hn, wg_ref[0, jj], preferred_element_type=F32)
        hu = jnp.dot(hn, wu_ref[0, jj], preferred_element_type=F32)
        acts.append(((hg * _sigmoid(hg)) * hu * comb_e).astype(BF16))
    n_f, n_d = wd_ref.shape[2:]
    o_ref[...] += jnp.dot(jnp.concatenate(acts, axis=-1), wd_ref[0].reshape(eps * n_f, n_d),
                          preferred_element_type=F32)

    if final_norm:
        @pl.when(step == pl.num_programs(1) - 1)
        def _():
            o_ref[...] = _rmsnorm(o_ref[...], fn_ref[...])


def _moe_layer(h, layer, g, wgr, bgr, wer, ber, wg, wu, wd, fn=None):
    T, D = h.shape
    _, n_e, _, F = wg.shape
    n_g = wgr.shape[-1]
    tm = min(MOE_TOKEN_TILE, T)
    eps = MOE_EXPERTS_PER_STEP
    assert T % tm == 0 and n_e % eps == 0 and n_e + n_g <= ROUTER_LO_LANE
    final_norm = fn is not None
    if fn is None:
        fn = jnp.ones((D,), F32)
    wr_hi, wr_lo = _split_bf16(jnp.concatenate([wer, wgr], axis=1), 2)
    pad = lambda a, n: jnp.pad(a, ((0, 0), (0, n - a.shape[1])))
    wr_cat = jnp.concatenate([pad(wr_hi, ROUTER_LO_LANE), pad(wr_lo, LANES - ROUTER_LO_LANE)], axis=1)
    const2 = lambda i, e: (0, 0)
    full = lambda a: pl.BlockSpec(a.shape, const2)
    small = [g.reshape(1, D), wr_cat, pad(wr_hi, LANES), bgr.reshape(n_g, 1), ber.reshape(n_e, 1)]
    return pl.pallas_call(
        functools.partial(_moe_kernel, final_norm=final_norm),
        grid=(T // tm, n_e // eps),
        in_specs=[pl.BlockSpec((tm, D), lambda i, e: (i, 0))] + [full(a) for a in small] + [
            pl.BlockSpec((1, eps, D, F), lambda i, e: (layer, e, 0, 0)),
            pl.BlockSpec((1, eps, D, F), lambda i, e: (layer, e, 0, 0)),
            pl.BlockSpec((1, eps, F, D), lambda i, e: (layer, e, 0, 0)),
            pl.BlockSpec((1, D), const2),
        ],
        out_specs=pl.BlockSpec((tm, D), lambda i, e: (i, 0)),
        out_shape=jax.ShapeDtypeStruct((T, D), F32),
        scratch_shapes=[pltpu.VMEM((tm, D), BF16), pltpu.VMEM((tm, LANES), F32)],
        compiler_params=pltpu.CompilerParams(
            dimension_semantics=("arbitrary", "arbitrary"), vmem_limit_bytes=VMEM_LIMIT_BYTES),
        name="moe_final" if final_norm else "moe",
    )(h, *small, wg, wu, wd, fn.reshape(1, D))


def _mlstm_kernel(h_ref, g_ref, wqk_ref, wv_t_ref, wo_t_ref, wgate_t_ref, convw_ref, convb_ref,
                  bi_t_ref, bf_t_ref, hnorm_ref, wout_t_ref, o_ref,
                  carry_ref, c_ref, m_ref):
    @pl.when(pl.program_id(1) == 0)
    def _():
        carry_ref[...] = jnp.zeros_like(carry_ref)
        c_ref[...] = jnp.zeros_like(c_ref)
        m_ref[...] = jnp.zeros_like(m_ref)

    for bb in range(h_ref.shape[0]):
        _mlstm_chunk(bb, h_ref, g_ref, wqk_ref, wv_t_ref, wo_t_ref, wgate_t_ref, convw_ref, convb_ref,
                     bi_t_ref, bf_t_ref, hnorm_ref, wout_t_ref, o_ref, carry_ref, c_ref, m_ref)


def _mlstm_chunk(bb, h_ref, g_ref, wqk_ref, wv_t_ref, wo_t_ref, wgate_t_ref, convw_ref, convb_ref,
                 bi_t_ref, bf_t_ref, hnorm_ref, wout_t_ref, o_ref, carry_ref, c_ref, m_ref):
    L = h_ref.shape[1]
    H = MLSTM_HEADS
    qk2 = convw_ref.shape[1]
    v_w = wout_t_ref.shape[1]
    dqk = qk2 // (2 * H)
    dv = v_w // H
    pw = 2 * dv
    ext_rows = c_ref.shape[2]
    assert 2 * dqk == LANES and dv == LANES and ext_rows == pw + STATE_EXTRA_ROWS
    nt = (((1,), (1,)), ((), ()))

    x = h_ref[bb]
    hn = _rmsnorm(x, g_ref[...])
    hb = hn.astype(BF16)
    gates_t = lax.dot_general(wgate_t_ref[...], hb, nt, preferred_element_type=F32)
    gates_t = gates_t[:2 * H, :] + gates_t[2 * H:, :]
    li_t = gates_t[:H, :] + bi_t_ref[...]
    lf_t = _log_sigmoid(gates_t[H:, :] + bf_t_ref[...])
    key = lax.broadcasted_iota(jnp.int32, (L, L), 0)
    qry = lax.broadcasted_iota(jnp.int32, (L, L), 1)
    causal_t = key <= qry
    b_t = sum(jnp.dot(part, causal_t.astype(BF16), preferred_element_type=F32)
              for part in _split_bf16(lf_t, 3))
    r_t = b_t - li_t
    r = jnp.concatenate([r_t, jnp.zeros((LANES - H, L), F32)], axis=0).T

    lane_t = lax.broadcasted_iota(jnp.int32, (1, L), 1)
    neg_inf = jnp.float32(-jnp.inf)
    best = li_t - b_t
    shift = 1
    while shift < L:
        best = jnp.maximum(best, jnp.where(lane_t >= shift, pltpu.roll(best, shift, 1), neg_inf))
        shift *= 2
    m_prev = m_ref[bb]
    inter_t = b_t + m_prev
    m_t = jnp.maximum(inter_t, b_t + best)
    s_inter_t = jnp.exp(inter_t - m_t)
    u_t = b_t - m_t
    clamp_t = jnp.exp(-m_t)
    b_last = b_t[:, L - 1:L]
    gsum_t = b_last - b_t + li_t
    m_new = jnp.maximum(b_last + m_prev, jnp.max(gsum_t, axis=1, keepdims=True))
    w_s_t = jnp.exp(gsum_t - m_new)
    decay = jnp.exp(b_last + m_prev - m_new)
    m_ref[bb] = m_new

    first = lax.broadcasted_iota(jnp.int32, (1, LANES), 1) < dqk
    erow = lax.broadcasted_iota(jnp.int32, (ext_rows, 1), 0)
    rows_a = (erow < dv) | (erow == pw)
    rows_b = ((erow >= dv) & (erow < pw)) | (erow == pw + 1)
    own_lanes = (rows_a & first) | (rows_b & jnp.logical_not(first))
    xrow = lax.broadcasted_iota(jnp.int32, (STATE_EXTRA_ROWS, 2 * L), 0)
    xcol = lax.broadcasted_iota(jnp.int32, (STATE_EXTRA_ROWS, 2 * L), 1)
    ones_rows = (((xrow == 0) & (xcol < L)) | ((xrow == 1) & (xcol >= L))).astype(BF16)
    zeros_t = jnp.zeros((dv, L), BF16)
    zeros_x = jnp.zeros((STATE_EXTRA_ROWS - 2, L), F32)
    acc_t = None
    for p in range(H // 2):
        heads = (2 * p, 2 * p + 1)
        qk_cols = slice(p * pw, (p + 1) * pw)
        qk_pre = jnp.dot(hb, wqk_ref[:, qk_cols], preferred_element_type=F32)
        v_t = lax.dot_general(wv_t_ref[qk_cols, :], hb, nt, preferred_element_type=F32)
        o_t = lax.dot_general(wo_t_ref[qk_cols, :], hb, nt, preferred_element_type=F32)

        ext = jnp.concatenate([carry_ref[bb, :, qk_cols], qk_pre], axis=0)
        carry_ref[bb, :, qk_cols] = qk_pre[L - CONV_CARRY:, :]
        conv = ext * convw_ref[CONV_K - 1:CONV_K, qk_cols]
        for d in range(1, CONV_K):
            conv = conv + pltpu.roll(ext, d, 0) * convw_ref[CONV_K - 1 - d:CONV_K - d, qk_cols]
        qk = conv[CONV_CARRY:, :] + convb_ref[:, qk_cols]
        qk = qk * _sigmoid(qk)
        q2 = qk[:, :LANES].astype(BF16)
        k2 = (qk[:, LANES:] * (dqk ** -0.5)).astype(BF16)

        c_ext = c_ref[bb, p]
        k_stack = jnp.concatenate([jnp.where(first, k2, jnp.zeros_like(k2)),
                                   jnp.where(first, jnp.zeros_like(k2), k2)], axis=0)
        s_t = lax.dot_general(k_stack, q2, nt, preferred_element_type=F32)
        inter_ext = lax.dot_general(c_ext.astype(BF16), q2, nt, preferred_element_type=F32)
        scs = []
        for half, hh in enumerate(heads):
            p_t = jnp.exp(jnp.where(causal_t, u_t[hh:hh + 1, :] - r[:, hh:hh + 1], neg_inf))
            scs.append((s_t[half * L:(half + 1) * L, :] * p_t).astype(BF16))
        vb = v_t.astype(BF16)
        v_ext = jnp.concatenate([jnp.concatenate([vb[:dv], zeros_t], axis=1),
                                 jnp.concatenate([zeros_t, vb[dv:]], axis=1), ones_rows], axis=0)
        num_ext = jnp.dot(v_ext, jnp.concatenate(scs, axis=0), preferred_element_type=F32)
        outs = []
        for half, hh in enumerate(heads):
            hs = slice(half * dv, (half + 1) * dv)
            si = s_inter_t[hh:hh + 1, :]
            num = num_ext[hs] + si * inter_ext[hs]
            den = num_ext[pw + half:pw + half + 1] + si * inter_ext[pw + half:pw + half + 1]
            ho = num * (1.0 / jnp.maximum(jnp.abs(den), clamp_t[hh:hh + 1, :]))
            outs.append(ho * lax.rsqrt(jnp.mean(ho * ho, axis=0, keepdims=True) + EPS))
        hnorm = hnorm_ref[qk_cols, :]
        gated_t = (jnp.concatenate(outs, axis=0) * jnp.concatenate([hnorm] * (L // LANES), axis=1)
                   * _sigmoid(o_t))
        proj_t = jnp.dot(wout_t_ref[:, qk_cols], gated_t.astype(BF16), preferred_element_type=F32)
        acc_t = proj_t if acc_t is None else acc_t + proj_t

        ws_a, ws_b = (w_s_t[hh:hh + 1, :] for hh in heads)
        wsv_ext = jnp.concatenate([ws_a * v_t[:dv], ws_b * v_t[dv:], ws_a, ws_b, zeros_x], axis=0).astype(BF16)
        upd = jnp.dot(wsv_ext, k2, preferred_element_type=F32)
        d_a, d_b = (decay[hh:hh + 1, :] for hh in heads)
        c_ref[bb, p] = jnp.where(rows_a, d_a, d_b) * c_ext + jnp.where(own_lanes, upd, 0.0)
    o_ref[bb] = x + acc_t.T


def _pair_major(a, n_pairs):
    half = a.shape[-1] // 2
    blk = half // n_pairs
    parts = []
    for p in range(n_pairs):
        parts += [a[..., p * blk:(p + 1) * blk], a[..., half + p * blk:half + (p + 1) * blk]]
    return jnp.concatenate(parts, axis=-1)


def _mlstm_layer(h, g, w_in, conv_w, conv_b, b_i, b_f, head_norm, w_out):
    B, S, D = h.shape
    H = MLSTM_HEADS
    v_w = w_out.shape[0]
    qk2 = conv_w.shape[1]
    assert w_in.shape[1] == qk2 + 2 * v_w + 2 * H and qk2 == v_w
    L = min(MLSTM_CHUNK, S)
    nb = MLSTM_BATCH_PER_STEP
    assert S % L == 0 and B % nb == 0 and L % LANES == 0
    w_gate = jnp.concatenate(_split_bf16(w_in[:, qk2 + 2 * v_w:], 2), axis=1)
    const2 = lambda b, j: (0, 0)
    full = lambda a: pl.BlockSpec(a.shape, const2)
    operands = [g.reshape(1, D), _pair_major(w_in[:, :qk2], H // 2).astype(BF16),
                w_in[:, qk2:qk2 + v_w].T.astype(BF16), w_in[:, qk2 + v_w:qk2 + 2 * v_w].T.astype(BF16),
                w_gate.T, _pair_major(conv_w, H // 2), _pair_major(conv_b.reshape(1, qk2), H // 2),
                b_i.reshape(H, 1), b_f.reshape(H, 1),
                jnp.broadcast_to(head_norm.reshape(v_w, 1), (v_w, LANES)), w_out.T.astype(BF16)]
    return pl.pallas_call(
        _mlstm_kernel,
        grid=(B // nb, S // L),
        in_specs=[pl.BlockSpec((nb, L, D), lambda b, j: (b, j, 0))] + [full(a) for a in operands],
        out_specs=pl.BlockSpec((nb, L, D), lambda b, j: (b, j, 0)),
        out_shape=jax.ShapeDtypeStruct((B, S, D), F32),
        scratch_shapes=[
            pltpu.VMEM((nb, CONV_CARRY, qk2), F32),
            pltpu.VMEM((nb, H // 2, 2 * v_w // H + STATE_EXTRA_ROWS, LANES), F32),
            pltpu.VMEM((nb, H, 1), F32),
        ],
        compiler_params=pltpu.CompilerParams(
            dimension_semantics=("arbitrary", "arbitrary"), vmem_limit_bytes=VMEM_LIMIT_BYTES),
        name="mlstm_mixer",
    )(h, *operands)


def kernel(x, mix_norm, pool_w, pool_scale, w_in, conv_w, conv_b, gate_bias_i, gate_bias_f, head_norm,
           w_out, ffn_norm, w_group_router, b_group_router, w_expert_router, b_expert_router, w_gate,
           w_up, w_down, final_norm):
    B, S, D = x.shape
    depth = mix_norm.shape[0]
    n_mixers = 2
    wg_b, wu_b, wd_b = w_gate.astype(BF16), w_up.astype(BF16), w_down.astype(BF16)
    h = x
    for i in range(depth):
        j = i // n_mixers
        if i % n_mixers == 0:
            h = _pool_layer(h, mix_norm[i], pool_w[j], pool_scale[j])
        else:
            h = _mlstm_layer(h, mix_norm[i], w_in[j], conv_w[j], conv_b[j], gate_bias_i[j],
                             gate_bias_f[j], head_norm[j], w_out[j])
        fn = final_norm if i == depth - 1 else None
        h = _moe_layer(h.reshape(B * S, D), i, ffn_norm[i], w_group_router[i], b_group_router[i],
                       w_expert_router[i], b_expert_router[i], wg_b, wu_b, wd_b, fn).reshape(B, S, D)
    return h
```

```python
import functools

import jax
import jax.numpy as jnp
from jax import lax
from jax.experimental import pallas as pl
from jax.experimental.pallas import tpu as pltpu

EPS = 1e-6
POOL_WINDOWS = (2, 4, 8, 16)
MLSTM_HEADS = 8
CONV_K = 4

SUBLANES = 8
LANES = 128
VMEM_LIMIT_BYTES = 56 * 1024 * 1024

POOL_SEQ_TILE = 512
POOL_HALO = 16
MOE_TOKEN_TILE = 1024
MOE_EXPERTS_PER_STEP = 4
ROUTER_LO_LANE = 32
MLSTM_CHUNK = 256
MLSTM_BATCH_PER_STEP = 4
CONV_CARRY = SUBLANES
STATE_EXTRA_ROWS = 16

F32 = jnp.float32
BF16 = jnp.bfloat16


def _rmsnorm(v, g):
    return (v * lax.rsqrt(jnp.mean(v * v, axis=-1, keepdims=True) + EPS)) * g


def _sigmoid(v):
    return 1.0 / (1.0 + jnp.exp(-v))


def _log_sigmoid(v):
    return jnp.minimum(v, 0.0) - jnp.log1p(jnp.exp(-jnp.abs(v)))


def _split_bf16(a, parts):
    out = []
    for _ in range(parts - 1):
        hi = a.astype(BF16)
        out.append(hi)
        a = a - hi.astype(F32)
    out.append(a.astype(BF16))
    return out


def _pool_kernel(x_ref, halo_ref, g_ref, w_ref, scale_ref, o_ref):
    j = pl.program_id(1)
    ts = x_ref.shape[1]
    gc = w_ref.shape[1]
    x = x_ref[0]
    g = g_ref[...]
    hn = _rmsnorm(x, g)
    hh = _rmsnorm(halo_ref[0], g) * (j > 0).astype(F32)
    s = jnp.concatenate([hh, hn], axis=0)
    sums = []
    shift = 1
    for _ in POOL_WINDOWS:
        s = s + pltpu.roll(s, shift, 0)
        sums.append(s[POOL_HALO:, :gc])
        s = s[:, gc:]
        shift *= 2
    t = (j * ts + lax.broadcasted_iota(jnp.int32, (ts, 1), 0) + 1).astype(F32)
    ys = []
    for k, w in enumerate(POOL_WINDOWS):
        count = jnp.minimum(t, float(w))
        pooled = sums[k] / count - hn[:, k * gc:(k + 1) * gc]
        ys.append(jnp.dot(pooled.astype(BF16), w_ref[k], preferred_element_type=F32))
    y = jnp.concatenate(ys, axis=-1)
    o_ref[0] = x + y * scale_ref[...]


def _pool_layer(x, g, w, scale):
    B, S, D = x.shape
    ts = min(POOL_SEQ_TILE, S)
    assert S % ts == 0 and ts % POOL_HALO == 0
    n_groups, gc, _ = w.shape
    assert n_groups == len(POOL_WINDOWS) and n_groups * gc == D
    halo_blocks = ts // POOL_HALO
    return pl.pallas_call(
        _pool_kernel,
        grid=(B, S // ts),
        in_specs=[
            pl.BlockSpec((1, ts, D), lambda b, j: (b, j, 0)),
            pl.BlockSpec((1, POOL_HALO, D), lambda b, j: (b, jnp.maximum(j * halo_blocks - 1, 0), 0)),
            pl.BlockSpec((1, D), lambda b, j: (0, 0)),
            pl.BlockSpec((n_groups, gc, gc), lambda b, j: (0, 0, 0)),
            pl.BlockSpec((1, D), lambda b, j: (0, 0)),
        ],
        out_specs=pl.BlockSpec((1, ts, D), lambda b, j: (b, j, 0)),
        out_shape=jax.ShapeDtypeStruct((B, S, D), F32),
        compiler_params=pltpu.CompilerParams(
            dimension_semantics=("arbitrary", "arbitrary"), vmem_limit_bytes=VMEM_LIMIT_BYTES),
        name="pool_mixer",
    )(x, x, g.reshape(1, D), w.astype(BF16), scale.reshape(1, D))


def _route_t(logits_t, bgr, ber):
    n_e, n_g = ber.shape[0], bgr.shape[0]
    el = logits_t[:n_e, :] + ber
    gl = logits_t[n_e:n_e + n_g, :] + bgr
    gidx = lax.broadcasted_iota(jnp.int32, gl.shape, 0)
    gmax = jnp.max(gl, axis=0, keepdims=True)
    gsel = jnp.min(jnp.where(gl == gmax, gidx, n_g), axis=0, keepdims=True)
    pg = 1.0 / jnp.sum(jnp.exp(gl - gmax), axis=0, keepdims=True)
    eidx = lax.broadcasted_iota(jnp.int32, el.shape, 0)
    in_group = (eidx // (n_e // n_g)) == gsel
    neg_inf = jnp.float32(-jnp.inf)
    cand = jnp.where(in_group, el, neg_inf)
    v1 = jnp.max(cand, axis=0, keepdims=True)
    i1 = jnp.min(jnp.where(cand == v1, eidx, n_e), axis=0, keepdims=True)
    cand2 = jnp.where(eidx == i1, neg_inf, cand)
    v2 = jnp.max(cand2, axis=0, keepdims=True)
    i2 = jnp.min(jnp.where(cand2 == v2, eidx, n_e), axis=0, keepdims=True)
    r = jnp.exp(v2 - v1)
    w1 = pg / (1.0 + r)
    w2 = pg * r / (1.0 + r)
    return jnp.where(eidx == i1, w1, 0.0) + jnp.where(eidx == i2, w2, 0.0)


def _moe_kernel(h_ref, g_ref, wrc_ref, wrh_ref, bgr_ref, ber_ref, wg_ref, wu_ref, wd_ref, fn_ref, *rest,
                final_norm, n_cast):
    cast_in, (o_ref, *cast_out), (hn_ref, comb_ref) = rest[:n_cast], rest[n_cast:2 * n_cast + 1], rest[2 * n_cast + 1:]
    for src, dst in zip(cast_in, cast_out):
        dst[...] = src[...].astype(BF16)
    step = pl.program_id(1)
    eps = wg_ref.shape[1]
    tm = h_ref.shape[0]
    n_f, n_d = wd_ref.shape[2:]

    def experts(hn, comb):
        eidx = lax.broadcasted_iota(jnp.int32, comb.shape, 1)
        acts = []
        for jj in range(eps):
            comb_e = jnp.sum(jnp.where(eidx == step * eps + jj, comb, 0.0), axis=-1, keepdims=True)
            hg = jnp.dot(hn, wg_ref[0, jj], preferred_element_type=F32)
            hu = jnp.dot(hn, wu_ref[0, jj], preferred_element_type=F32)
            acts.append(((hg * _sigmoid(hg)) * hu * comb_e).astype(BF16))
        return jnp.dot(jnp.concatenate(acts, axis=-1), wd_ref[0].reshape(eps * n_f, n_d),
                       preferred_element_type=F32)

    @pl.when(step == 0)
    def _():
        x = h_ref[...]
        hn, hn_lo = _split_bf16(_rmsnorm(x, g_ref[...]), 2)
        logits_t = (jnp.dot(hn, wrc_ref[...], preferred_element_type=F32)
                    + jnp.dot(hn_lo, wrh_ref[...], preferred_element_type=F32)).T
        n_l = ROUTER_LO_LANE
        comb_t = _route_t(logits_t[:n_l, :] + logits_t[n_l:2 * n_l, :], bgr_ref[...], ber_ref[...])
        comb_t = jnp.concatenate([comb_t, jnp.zeros((LANES - comb_t.shape[0], tm), F32)], axis=0)
        comb = comb_t.T
        comb_ref[...] = comb
        hn_ref[...] = hn
        o_ref[...] = x + experts(hn, comb)

    @pl.when(step > 0)
    def _():
        o_ref[...] += experts(hn_ref[...], comb_ref[...])

    if final_norm:
        @pl.when(step == pl.num_programs(1) - 1)
        def _():
            o_ref[...] = _rmsnorm(o_ref[...], fn_ref[...])


def _moe_layer(h, g, wgr, bgr, wer, ber, wg, wu, wd, fn=None, cast_next=None):
    T, D = h.shape
    _, n_e, _, F = wg.shape
    n_g = wgr.shape[-1]
    tm = min(MOE_TOKEN_TILE, T)
    eps = MOE_EXPERTS_PER_STEP
    assert T % tm == 0 and n_e % eps == 0 and n_e + n_g <= ROUTER_LO_LANE
    n_inner = n_e // eps
    final_norm = fn is not None
    if fn is None:
        fn = jnp.ones((D,), F32)
    cast_arrays, cast_in_specs, cast_out_specs, cast_out_shapes = [], [], [], []
    if cast_next is not None:
        nxt, cast_arrays = cast_next
        n_steps = (T // tm) * n_inner
        assert n_steps % n_e == 0
        slabs = n_steps // n_e
        slab_of = lambda i, e: ((i * n_inner + e) // slabs, (i * n_inner + e) % slabs)
        for w in cast_arrays:
            rows = w.shape[2] // slabs
            assert w.shape[1] == n_e and rows * slabs == w.shape[2] and rows % (2 * SUBLANES) == 0
            blk = (1, 1, rows, w.shape[3])
            cast_in_specs.append(pl.BlockSpec(blk, lambda i, e: (nxt, *slab_of(i, e), 0)))
            cast_out_specs.append(pl.BlockSpec(blk, lambda i, e: (0, *slab_of(i, e), 0)))
            cast_out_shapes.append(jax.ShapeDtypeStruct((1,) + w.shape[1:], BF16))
    wr_hi, wr_lo = _split_bf16(jnp.concatenate([wer, wgr], axis=1), 2)
    pad = lambda a, n: jnp.pad(a, ((0, 0), (0, n - a.shape[1])))
    wr_cat = jnp.concatenate([pad(wr_hi, ROUTER_LO_LANE), pad(wr_lo, LANES - ROUTER_LO_LANE)], axis=1)
    const2 = lambda i, e: (0, 0)
    full = lambda a: pl.BlockSpec(a.shape, const2)
    small = [g.reshape(1, D), wr_cat, pad(wr_hi, LANES), bgr.reshape(n_g, 1), ber.reshape(n_e, 1)]
    out, *cast = pl.pallas_call(
        functools.partial(_moe_kernel, final_norm=final_norm, n_cast=len(cast_arrays)),
        grid=(T // tm, n_inner),
        in_specs=[pl.BlockSpec((tm, D), lambda i, e: (i, 0))] + [full(a) for a in small] + [
            pl.BlockSpec((1, eps, D, F), lambda i, e: (0, e, 0, 0)),
            pl.BlockSpec((1, eps, D, F), lambda i, e: (0, e, 0, 0)),
            pl.BlockSpec((1, eps, F, D), lambda i, e: (0, e, 0, 0)),
            pl.BlockSpec((1, D), const2),
        ] + cast_in_specs,
        out_specs=[pl.BlockSpec((tm, D), lambda i, e: (i, 0))] + cast_out_specs,
        out_shape=[jax.ShapeDtypeStruct((T, D), F32)] + cast_out_shapes,
        scratch_shapes=[pltpu.VMEM((tm, D), BF16), pltpu.VMEM((tm, LANES), F32)],
        compiler_params=pltpu.CompilerParams(
            dimension_semantics=("arbitrary", "arbitrary"), vmem_limit_bytes=VMEM_LIMIT_BYTES),
        name="moe_final" if final_norm else "moe",
    )(h, *small, wg, wu, wd, fn.reshape(1, D), *cast_arrays)
    return out, cast


def _mlstm_kernel(h_ref, g_ref, wqk_ref, wv_t_ref, wo_t_ref, wgate_t_ref, convw_ref, convb_ref,
                  bi_t_ref, bf_t_ref, hnorm_ref, wout_t_ref, o_ref,
                  carry_ref, c_ref, m_ref):
    @pl.when(pl.program_id(1) == 0)
    def _():
        carry_ref[...] = jnp.zeros_like(carry_ref)
        c_ref[...] = jnp.zeros_like(c_ref)
        m_ref[...] = jnp.zeros_like(m_ref)

    for bb in range(h_ref.shape[0]):
        _mlstm_chunk(bb, h_ref, g_ref, wqk_ref, wv_t_ref, wo_t_ref, wgate_t_ref, convw_ref, convb_ref,
                     bi_t_ref, bf_t_ref, hnorm_ref, wout_t_ref, o_ref, carry_ref, c_ref, m_ref)


def _mlstm_chunk(bb, h_ref, g_ref, wqk_ref, wv_t_ref, wo_t_ref, wgate_t_ref, convw_ref, convb_ref,
                 bi_t_ref, bf_t_ref, hnorm_ref, wout_t_ref, o_ref, carry_ref, c_ref, m_ref):
    L = h_ref.shape[1]
    H = MLSTM_HEADS
    qk2 = convw_ref.shape[1]
    v_w = wout_t_ref.shape[1]
    dqk = qk2 // (2 * H)
    dv = v_w // H
    pw = 2 * dv
    ext_rows = c_ref.shape[2]
    assert 2 * dqk == LANES and dv == LANES and ext_rows == pw + STATE_EXTRA_ROWS
    nt = (((1,), (1,)), ((), ()))

    x = h_ref[bb]
    hn = _rmsnorm(x, g_ref[...])
    hb = hn.astype(BF16)
    gates_t = lax.dot_general(wgate_t_ref[...], hb, nt, preferred_element_type=F32)
    gates_t = gates_t[:2 * H, :] + gates_t[2 * H:, :]
    li_t = gates_t[:H, :] + bi_t_ref[...]
    lf_t = _log_sigmoid(gates_t[H:, :] + bf_t_ref[...])
    key = lax.broadcasted_iota(jnp.int32, (L, L), 0)
    qry = lax.broadcasted_iota(jnp.int32, (L, L), 1)
    causal_t = key <= qry
    b_t = sum(jnp.dot(part, causal_t.astype(BF16), preferred_element_type=F32)
              for part in _split_bf16(lf_t, 3))
    r_t = b_t - li_t
    r = jnp.concatenate([r_t, jnp.zeros((LANES - H, L), F32)], axis=0).T

    lane_t = lax.broadcasted_iota(jnp.int32, (1, L), 1)
    neg_inf = jnp.float32(-jnp.inf)
    best = li_t - b_t
    shift = 1
    while shift < L:
        best = jnp.maximum(best, jnp.where(lane_t >= shift, pltpu.roll(best, shift, 1), neg_inf))
        shift *= 2
    m_prev = m_ref[bb]
    inter_t = b_t + m_prev
    m_t = jnp.maximum(inter_t, b_t + best)
    s_inter_t = jnp.exp(inter_t - m_t)
    u_t = b_t - m_t
    clamp_t = jnp.exp(-m_t)
    b_last = b_t[:, L - 1:L]
    gsum_t = b_last - b_t + li_t
    m_new = jnp.maximum(b_last + m_prev, jnp.max(gsum_t, axis=1, keepdims=True))
    w_s_t = jnp.exp(gsum_t - m_new)
    decay = jnp.exp(b_last + m_prev - m_new)
    m_ref[bb] = m_new

    first = lax.broadcasted_iota(jnp.int32, (1, LANES), 1) < dqk
    erow = lax.broadcasted_iota(jnp.int32, (ext_rows, 1), 0)
    rows_a = (erow < dv) | (erow == pw)
    rows_b = ((erow >= dv) & (erow < pw)) | (erow == pw + 1)
    own_lanes = (rows_a & first) | (rows_b & jnp.logical_not(first))
    xrow = lax.broadcasted_iota(jnp.int32, (STATE_EXTRA_ROWS, 2 * L), 0)
    xcol = lax.broadcasted_iota(jnp.int32, (STATE_EXTRA_ROWS, 2 * L), 1)
    ones_rows = (((xrow == 0) & (xcol < L)) | ((xrow == 1) & (xcol >= L))).astype(BF16)
    zeros_t = jnp.zeros((dv, L), BF16)
    zeros_x = jnp.zeros((STATE_EXTRA_ROWS - 2, L), F32)
    acc_t = None
    for p in range(H // 2):
        heads = (2 * p, 2 * p + 1)
        qk_cols = slice(p * pw, (p + 1) * pw)
        qk_pre = jnp.dot(hb, wqk_ref[:, qk_cols], preferred_element_type=F32)
        v_t = lax.dot_general(wv_t_ref[qk_cols, :], hb, nt, preferred_element_type=F32)
        o_t = lax.dot_general(wo_t_ref[qk_cols, :], hb, nt, preferred_element_type=F32)

        ext = jnp.concatenate([carry_ref[bb, :, qk_cols], qk_pre], axis=0)
        carry_ref[bb, :, qk_cols] = qk_pre[L - CONV_CARRY:, :]
        conv = ext * convw_ref[CONV_K - 1:CONV_K, qk_cols]
        for d in range(1, CONV_K):
            conv = conv + pltpu.roll(ext, d, 0) * convw_ref[CONV_K - 1 - d:CONV_K - d, qk_cols]
        qk = conv[CONV_CARRY:, :] + convb_ref[:, qk_cols]
        qk = qk * _sigmoid(qk)
        q2 = qk[:, :LANES].astype(BF16)
        k2 = (qk[:, LANES:] * (dqk ** -0.5)).astype(BF16)

        c_ext = c_ref[bb, p]
        k_stack = jnp.concatenate([jnp.where(first, k2, jnp.zeros_like(k2)),
                                   jnp.where(first, jnp.zeros_like(k2), k2)], axis=0)
        s_t = lax.dot_general(k_stack, q2, nt, preferred_element_type=F32)
        inter_ext = lax.dot_general(c_ext.astype(BF16), q2, nt, preferred_element_type=F32)
        scs = []
        for half, hh in enumerate(heads):
            p_t = jnp.exp(jnp.where(causal_t, u_t[hh:hh + 1, :] - r[:, hh:hh + 1], neg_inf))
            scs.append((s_t[half * L:(half + 1) * L, :] * p_t).astype(BF16))
        vb = v_t.astype(BF16)
        v_ext = jnp.concatenate([jnp.concatenate([vb[:dv], zeros_t], axis=1),
                                 jnp.concatenate([zeros_t, vb[dv:]], axis=1), ones_rows], axis=0)
        num_ext = jnp.dot(v_ext, jnp.concatenate(scs, axis=0), preferred_element_type=F32)
        outs = []
        for half, hh in enumerate(heads):
            hs = slice(half * dv, (half + 1) * dv)
            si = s_inter_t[hh:hh + 1, :]
            num = num_ext[hs] + si * inter_ext[hs]
            den = num_ext[pw + half:pw + half + 1] + si * inter_ext[pw + half:pw + half + 1]
            ho = num * (1.0 / jnp.maximum(jnp.abs(den), clamp_t[hh:hh + 1, :]))
            outs.append(ho * lax.rsqrt(jnp.mean(ho * ho, axis=0, keepdims=True) + EPS))
        hnorm = hnorm_ref[qk_cols, :]
        gated_t = (jnp.concatenate(outs, axis=0) * jnp.concatenate([hnorm] * (L // LANES), axis=1)
                   * _sigmoid(o_t))
        proj_t = jnp.dot(wout_t_ref[:, qk_cols], gated_t.astype(BF16), preferred_element_type=F32)
        acc_t = proj_t if acc_t is None else acc_t + proj_t

        ws_a, ws_b = (w_s_t[hh:hh + 1, :] for hh in heads)
        wsv_ext = jnp.concatenate([ws_a * v_t[:dv], ws_b * v_t[dv:], ws_a, ws_b, zeros_x], axis=0).astype(BF16)
        upd = jnp.dot(wsv_ext, k2, preferred_element_type=F32)
        d_a, d_b = (decay[hh:hh + 1, :] for hh in heads)
        c_ref[bb, p] = jnp.where(rows_a, d_a, d_b) * c_ext + jnp.where(own_lanes, upd, 0.0)
    o_ref[bb] = x + acc_t.T


def _pair_major(a, n_pairs):
    half = a.shape[-1] // 2
    blk = half // n_pairs
    parts = []
    for p in range(n_pairs):
        parts += [a[..., p * blk:(p + 1) * blk], a[..., half + p * blk:half + (p + 1) * blk]]
    return jnp.concatenate(parts, axis=-1)


def _mlstm_layer(h, g, w_in, conv_w, conv_b, b_i, b_f, head_norm, w_out):
    B, S, D = h.shape
    H = MLSTM_HEADS
    v_w = w_out.shape[0]
    qk2 = conv_w.shape[1]
    assert w_in.shape[1] == qk2 + 2 * v_w + 2 * H and qk2 == v_w
    L = min(MLSTM_CHUNK, S)
    nb = MLSTM_BATCH_PER_STEP
    assert S % L == 0 and B % nb == 0 and L % LANES == 0
    w_gate = jnp.concatenate(_split_bf16(w_in[:, qk2 + 2 * v_w:], 2), axis=1)
    const2 = lambda b, j: (0, 0)
    full = lambda a: pl.BlockSpec(a.shape, const2)
    operands = [g.reshape(1, D), _pair_major(w_in[:, :qk2], H // 2).astype(BF16),
                w_in[:, qk2:qk2 + v_w].T.astype(BF16), w_in[:, qk2 + v_w:qk2 + 2 * v_w].T.astype(BF16),
                w_gate.T, _pair_major(conv_w, H // 2), _pair_major(conv_b.reshape(1, qk2), H // 2),
                b_i.reshape(H, 1), b_f.reshape(H, 1),
                jnp.broadcast_to(head_norm.reshape(v_w, 1), (v_w, LANES)), w_out.T.astype(BF16)]
    return pl.pallas_call(
        _mlstm_kernel,
        grid=(B // nb, S // L),
        in_specs=[pl.BlockSpec((nb, L, D), lambda b, j: (b, j, 0))] + [full(a) for a in operands],
        out_specs=pl.BlockSpec((nb, L, D), lambda b, j: (b, j, 0)),
        out_shape=jax.ShapeDtypeStruct((B, S, D), F32),
        scratch_shapes=[
            pltpu.VMEM((nb, CONV_CARRY, qk2), F32),
            pltpu.VMEM((nb, H // 2, 2 * v_w // H + STATE_EXTRA_ROWS, LANES), F32),
            pltpu.VMEM((nb, H, 1), F32),
        ],
        compiler_params=pltpu.CompilerParams(
            dimension_semantics=("arbitrary", "arbitrary"), vmem_limit_bytes=VMEM_LIMIT_BYTES),
        name="mlstm_mixer",
    )(h, *operands)


def kernel(x, mix_norm, pool_w, pool_scale, w_in, conv_w, conv_b, gate_bias_i, gate_bias_f, head_norm,
           w_out, ffn_norm, w_group_router, b_group_router, w_expert_router, b_expert_router, w_gate,
           w_up, w_down, final_norm):
    B, S, D = x.shape
    depth = mix_norm.shape[0]
    n_mixers = 2
    expert_w = (w_gate, w_up, w_down)
    expert_w_b = [w[:1].astype(BF16) for w in expert_w]
    h = x
    for i in range(depth):
        j = i // n_mixers
        if i % n_mixers == 0:
            h = _pool_layer(h, mix_norm[i], pool_w[j], pool_scale[j])
        else:
            h = _mlstm_layer(h, mix_norm[i], w_in[j], conv_w[j], conv_b[j], gate_bias_i[j],
                             gate_bias_f[j], head_norm[j], w_out[j])
        fn = final_norm if i == depth - 1 else None
        cast_next = (i + 1, expert_w) if i + 1 < depth else None
        h, expert_w_b = _moe_layer(h.reshape(B * S, D), ffn_norm[i], w_group_router[i], b_group_router[i],
                                   w_expert_router[i], b_expert_router[i], *expert_w_b, fn, cast_next)
        h = h.reshape(B, S, D)
    return h
```

```python
import functools

import jax
import jax.numpy as jnp
from jax import lax
from jax.experimental import pallas as pl
from jax.experimental.pallas import tpu as pltpu

EPS = 1e-6
POOL_WINDOWS = (2, 4, 8, 16)
MLSTM_HEADS = 8
CONV_K = 4

SUBLANES = 8
LANES = 128
VMEM_LIMIT_BYTES = 56 * 1024 * 1024

POOL_SEQ_TILE = 512
POOL_HALO = 16
MOE_TOKEN_TILE = 1024
MOE_EXPERTS_PER_STEP = 4
ROUTER_LO_LANE = 32
MLSTM_CHUNK = 256
MLSTM_BATCH_PER_STEP = 4
CONV_CARRY = SUBLANES
STATE_EXTRA_ROWS = 16

F32 = jnp.float32
BF16 = jnp.bfloat16


def _rmsnorm(v, g):
    return (v * lax.rsqrt(jnp.mean(v * v, axis=-1, keepdims=True) + EPS)) * g


def _sigmoid(v):
    return 1.0 / (1.0 + jnp.exp(-v))


def _log_sigmoid(v):
    return jnp.minimum(v, 0.0) - jnp.log1p(jnp.exp(-jnp.abs(v)))


def _split_bf16(a, parts):
    out = []
    for _ in range(parts - 1):
        hi = a.astype(BF16)
        out.append(hi)
        a = a - hi.astype(F32)
    out.append(a.astype(BF16))
    return out


def _side_cast_specs(cast, grid):
    if cast is None:
        return [], [], [], []
    layer, arrays = cast
    n_steps = grid[0] * grid[1]
    in_specs, out_specs, out_shapes = [], [], []
    for w in arrays:
        n_e = w.shape[1]
        assert n_steps % n_e == 0
        slabs = n_steps // n_e
        rows = w.shape[2] // slabs
        assert rows * slabs == w.shape[2] and rows % (2 * SUBLANES) == 0
        blk = (1, 1, rows, w.shape[3])
        slab_of = lambda i, j, slabs=slabs: ((i * grid[1] + j) // slabs, (i * grid[1] + j) % slabs)
        in_specs.append(pl.BlockSpec(blk, lambda i, j, f=slab_of: (layer, *f(i, j), 0)))
        out_specs.append(pl.BlockSpec(blk, lambda i, j, f=slab_of: (0, *f(i, j), 0)))
        out_shapes.append(jax.ShapeDtypeStruct((1,) + w.shape[1:], BF16))
    return list(arrays), in_specs, out_specs, out_shapes


def _side_cast(srcs, dsts):
    for src, dst in zip(srcs, dsts):
        dst[...] = src[...].astype(BF16)


def _pool_kernel(x_ref, halo_ref, g_ref, w_ref, scale_ref, *rest):
    n_cast = len(rest) // 2
    o_ref = rest[n_cast]
    _side_cast(rest[:n_cast], rest[n_cast + 1:])
    j = pl.program_id(1)
    ts = x_ref.shape[1]
    gc = w_ref.shape[1]
    x = x_ref[0]
    g = g_ref[...]
    hn = _rmsnorm(x, g)
    hh = _rmsnorm(halo_ref[0], g) * (j > 0).astype(F32)
    s = jnp.concatenate([hh, hn], axis=0)
    sums = []
    shift = 1
    for _ in POOL_WINDOWS:
        s = s + pltpu.roll(s, shift, 0)
        sums.append(s[POOL_HALO:, :gc])
        s = s[:, gc:]
        shift *= 2
    t = (j * ts + lax.broadcasted_iota(jnp.int32, (ts, 1), 0) + 1).astype(F32)
    ys = []
    for k, w in enumerate(POOL_WINDOWS):
        count = jnp.minimum(t, float(w))
        pooled = sums[k] / count - hn[:, k * gc:(k + 1) * gc]
        ys.append(jnp.dot(pooled.astype(BF16), w_ref[k], preferred_element_type=F32))
    y = jnp.concatenate(ys, axis=-1)
    o_ref[0] = x + y * scale_ref[...]


def _pool_layer(x, g, w, scale, cast=None):
    B, S, D = x.shape
    ts = min(POOL_SEQ_TILE, S)
    assert S % ts == 0 and ts % POOL_HALO == 0
    n_groups, gc, _ = w.shape
    assert n_groups == len(POOL_WINDOWS) and n_groups * gc == D
    halo_blocks = ts // POOL_HALO
    cast_arrays, cast_in_specs, cast_out_specs, cast_out_shapes = _side_cast_specs(cast, (B, S // ts))
    out, *casted = pl.pallas_call(
        _pool_kernel,
        grid=(B, S // ts),
        in_specs=[
            pl.BlockSpec((1, ts, D), lambda b, j: (b, j, 0)),
            pl.BlockSpec((1, POOL_HALO, D), lambda b, j: (b, jnp.maximum(j * halo_blocks - 1, 0), 0)),
            pl.BlockSpec((1, D), lambda b, j: (0, 0)),
            pl.BlockSpec((n_groups, gc, gc), lambda b, j: (0, 0, 0)),
            pl.BlockSpec((1, D), lambda b, j: (0, 0)),
        ] + cast_in_specs,
        out_specs=[pl.BlockSpec((1, ts, D), lambda b, j: (b, j, 0))] + cast_out_specs,
        out_shape=[jax.ShapeDtypeStruct((B, S, D), F32)] + cast_out_shapes,
        compiler_params=pltpu.CompilerParams(
            dimension_semantics=("arbitrary", "arbitrary"), vmem_limit_bytes=VMEM_LIMIT_BYTES),
        name="pool_mixer",
    )(x, x, g.reshape(1, D), w.astype(BF16), scale.reshape(1, D), *cast_arrays)
    return out, casted


def _route_t(logits_t, bgr, ber):
    n_e, n_g = ber.shape[0], bgr.shape[0]
    el = logits_t[:n_e, :] + ber
    gl = logits_t[n_e:n_e + n_g, :] + bgr
    gidx = lax.broadcasted_iota(jnp.int32, gl.shape, 0)
    gmax = jnp.max(gl, axis=0, keepdims=True)
    gsel = jnp.min(jnp.where(gl == gmax, gidx, n_g), axis=0, keepdims=True)
    pg = 1.0 / jnp.sum(jnp.exp(gl - gmax), axis=0, keepdims=True)
    eidx = lax.broadcasted_iota(jnp.int32, el.shape, 0)
    in_group = (eidx // (n_e // n_g)) == gsel
    neg_inf = jnp.float32(-jnp.inf)
    cand = jnp.where(in_group, el, neg_inf)
    v1 = jnp.max(cand, axis=0, keepdims=True)
    i1 = jnp.min(jnp.where(cand == v1, eidx, n_e), axis=0, keepdims=True)
    cand2 = jnp.where(eidx == i1, neg_inf, cand)
    v2 = jnp.max(cand2, axis=0, keepdims=True)
    i2 = jnp.min(jnp.where(cand2 == v2, eidx, n_e), axis=0, keepdims=True)
    r = jnp.exp(v2 - v1)
    w1 = pg / (1.0 + r)
    w2 = pg * r / (1.0 + r)
    return jnp.where(eidx == i1, w1, 0.0) + jnp.where(eidx == i2, w2, 0.0)


def _moe_kernel(h_ref, g_ref, wrc_ref, wrh_ref, bgr_ref, ber_ref, wg_ref, wu_ref, wd_ref, fn_ref, *rest,
                final_norm, n_cast):
    cast_in, (o_ref, *cast_out), (hn_ref, comb_ref) = rest[:n_cast], rest[n_cast:2 * n_cast + 1], rest[2 * n_cast + 1:]
    _side_cast(cast_in, cast_out)
    step = pl.program_id(1)
    eps = wg_ref.shape[1]
    tm = h_ref.shape[0]
    n_f, n_d = wd_ref.shape[2:]

    def experts(hn, comb):
        eidx = lax.broadcasted_iota(jnp.int32, comb.shape, 1)
        acts = []
        for jj in range(eps):
            comb_e = jnp.sum(jnp.where(eidx == step * eps + jj, comb, 0.0), axis=-1, keepdims=True)
            hg = jnp.dot(hn, wg_ref[0, jj], preferred_element_type=F32)
            hu = jnp.dot(hn, wu_ref[0, jj], preferred_element_type=F32)
            acts.append(((hg * _sigmoid(hg)) * hu * comb_e).astype(BF16))
        return jnp.dot(jnp.concatenate(acts, axis=-1), wd_ref[0].reshape(eps * n_f, n_d),
                       preferred_element_type=F32)

    @pl.when(step == 0)
    def _():
        x = h_ref[...]
        hn, hn_lo = _split_bf16(_rmsnorm(x, g_ref[...]), 2)
        logits_t = (jnp.dot(hn, wrc_ref[...], preferred_element_type=F32)
                    + jnp.dot(hn_lo, wrh_ref[...], preferred_element_type=F32)).T
        n_l = ROUTER_LO_LANE
        comb_t = _route_t(logits_t[:n_l, :] + logits_t[n_l:2 * n_l, :], bgr_ref[...], ber_ref[...])
        comb_t = jnp.concatenate([comb_t, jnp.zeros((LANES - comb_t.shape[0], tm), F32)], axis=0)
        comb = comb_t.T
        comb_ref[...] = comb
        hn_ref[...] = hn
        o_ref[...] = x + experts(hn, comb)

    @pl.when(step > 0)
    def _():
        o_ref[...] += experts(hn_ref[...], comb_ref[...])

    if final_norm:
        @pl.when(step == pl.num_programs(1) - 1)
        def _():
            o_ref[...] = _rmsnorm(o_ref[...], fn_ref[...])


def _moe_layer(h, g, wgr, bgr, wer, ber, wg, wu, wd, fn=None, cast_next=None):
    T, D = h.shape
    _, n_e, _, F = wg.shape
    n_g = wgr.shape[-1]
    tm = min(MOE_TOKEN_TILE, T)
    eps = MOE_EXPERTS_PER_STEP
    assert T % tm == 0 and n_e % eps == 0 and n_e + n_g <= ROUTER_LO_LANE
    n_inner = n_e // eps
    final_norm = fn is not None
    if fn is None:
        fn = jnp.ones((D,), F32)
    cast_arrays, cast_in_specs, cast_out_specs, cast_out_shapes = _side_cast_specs(cast_next, (T // tm, n_inner))
    wr_hi, wr_lo = _split_bf16(jnp.concatenate([wer, wgr], axis=1), 2)
    pad = lambda a, n: jnp.pad(a, ((0, 0), (0, n - a.shape[1])))
    wr_cat = jnp.concatenate([pad(wr_hi, ROUTER_LO_LANE), pad(wr_lo, LANES - ROUTER_LO_LANE)], axis=1)
    const2 = lambda i, e: (0, 0)
    full = lambda a: pl.BlockSpec(a.shape, const2)
    small = [g.reshape(1, D), wr_cat, pad(wr_hi, LANES), bgr.reshape(n_g, 1), ber.reshape(n_e, 1)]
    out, *cast = pl.pallas_call(
        functools.partial(_moe_kernel, final_norm=final_norm, n_cast=len(cast_arrays)),
        grid=(T // tm, n_inner),
        in_specs=[pl.BlockSpec((tm, D), lambda i, e: (i, 0))] + [full(a) for a in small] + [
            pl.BlockSpec((1, eps, D, F), lambda i, e: (0, e, 0, 0)),
            pl.BlockSpec((1, eps, D, F), lambda i, e: (0, e, 0, 0)),
            pl.BlockSpec((1, eps, F, D), lambda i, e: (0, e, 0, 0)),
            pl.BlockSpec((1, D), const2),
        ] + cast_in_specs,
        out_specs=[pl.BlockSpec((tm, D), lambda i, e: (i, 0))] + cast_out_specs,
        out_shape=[jax.ShapeDtypeStruct((T, D), F32)] + cast_out_shapes,
        scratch_shapes=[pltpu.VMEM((tm, D), BF16), pltpu.VMEM((tm, LANES), F32)],
        compiler_params=pltpu.CompilerParams(
            dimension_semantics=("arbitrary", "arbitrary"), vmem_limit_bytes=VMEM_LIMIT_BYTES),
        name="moe_final" if final_norm else "moe",
    )(h, *small, wg, wu, wd, fn.reshape(1, D), *cast_arrays)
    return out, cast


def _mlstm_kernel(h_ref, g_ref, wqk_ref, wv_t_ref, wo_t_ref, wgate_t_ref, convw_ref, convb_ref,
                  bi_t_ref, bf_t_ref, hnorm_ref, wout_t_ref, o_ref,
                  carry_ref, c_ref, m_ref):
    @pl.when(pl.program_id(1) == 0)
    def _():
        carry_ref[...] = jnp.zeros_like(carry_ref)
        c_ref[...] = jnp.zeros_like(c_ref)
        m_ref[...] = jnp.zeros_like(m_ref)

    for bb in range(h_ref.shape[0]):
        _mlstm_chunk(bb, h_ref, g_ref, wqk_ref, wv_t_ref, wo_t_ref, wgate_t_ref, convw_ref, convb_ref,
                     bi_t_ref, bf_t_ref, hnorm_ref, wout_t_ref, o_ref, carry_ref, c_ref, m_ref)


def _mlstm_chunk(bb, h_ref, g_ref, wqk_ref, wv_t_ref, wo_t_ref, wgate_t_ref, convw_ref, convb_ref,
                 bi_t_ref, bf_t_ref, hnorm_ref, wout_t_ref, o_ref, carry_ref, c_ref, m_ref):
    L = h_ref.shape[1]
    H = MLSTM_HEADS
    qk2 = convw_ref.shape[1]
    v_w = wout_t_ref.shape[1]
    dqk = qk2 // (2 * H)
    dv = v_w // H
    pw = 2 * dv
    ext_rows = c_ref.shape[2]
    assert 2 * dqk == LANES and dv == LANES and ext_rows == pw + STATE_EXTRA_ROWS
    nt = (((1,), (1,)), ((), ()))

    x = h_ref[bb]
    hn = _rmsnorm(x, g_ref[...])
    hb = hn.astype(BF16)
    gates_t = lax.dot_general(wgate_t_ref[...], hb, nt, preferred_element_type=F32)
    gates_t = gates_t[:2 * H, :] + gates_t[2 * H:, :]
    li_t = gates_t[:H, :] + bi_t_ref[...]
    lf_t = _log_sigmoid(gates_t[H:, :] + bf_t_ref[...])
    key = lax.broadcasted_iota(jnp.int32, (L, L), 0)
    qry = lax.broadcasted_iota(jnp.int32, (L, L), 1)
    causal_t = key <= qry
    b_t = sum(jnp.dot(part, causal_t.astype(BF16), preferred_element_type=F32)
              for part in _split_bf16(lf_t, 3))
    r_t = b_t - li_t
    r = jnp.concatenate([r_t, jnp.zeros((LANES - H, L), F32)], axis=0).T

    lane_t = lax.broadcasted_iota(jnp.int32, (1, L), 1)
    neg_inf = jnp.float32(-jnp.inf)
    best = li_t - b_t
    shift = 1
    while shift < L:
        best = jnp.maximum(best, jnp.where(lane_t >= shift, pltpu.roll(best, shift, 1), neg_inf))
        shift *= 2
    m_prev = m_ref[bb]
    inter_t = b_t + m_prev
    m_t = jnp.maximum(inter_t, b_t + best)
    s_inter_t = jnp.exp(inter_t - m_t)
    u_t = b_t - m_t
    clamp_t = jnp.exp(-m_t)
    b_last = b_t[:, L - 1:L]
    gsum_t = b_last - b_t + li_t
    m_new = jnp.maximum(b_last + m_prev, jnp.max(gsum_t, axis=1, keepdims=True))
    w_s_t = jnp.exp(gsum_t - m_new)
    decay = jnp.exp(b_last + m_prev - m_new)
    m_ref[bb] = m_new

    first = lax.broadcasted_iota(jnp.int32, (1, LANES), 1) < dqk
    erow = lax.broadcasted_iota(jnp.int32, (ext_rows, 1), 0)
    rows_a = (erow < dv) | (erow == pw)
    rows_b = ((erow >= dv) & (erow < pw)) | (erow == pw + 1)
    own_lanes = (rows_a & first) | (rows_b & jnp.logical_not(first))
    xrow = lax.broadcasted_iota(jnp.int32, (STATE_EXTRA_ROWS, 2 * L), 0)
    xcol = lax.broadcasted_iota(jnp.int32, (STATE_EXTRA_ROWS, 2 * L), 1)
    ones_rows = (((xrow == 0) & (xcol < L)) | ((xrow == 1) & (xcol >= L))).astype(BF16)
    zeros_t = jnp.zeros((dv, L), BF16)
    zeros_x = jnp.zeros((STATE_EXTRA_ROWS - 2, L), F32)
    acc_t = None
    for p in range(H // 2):
        heads = (2 * p, 2 * p + 1)
        qk_cols = slice(p * pw, (p + 1) * pw)
        qk_pre = jnp.dot(hb, wqk_ref[:, qk_cols], preferred_element_type=F32)
        v_t = lax.dot_general(wv_t_ref[qk_cols, :], hb, nt, preferred_element_type=F32)
        o_t = lax.dot_general(wo_t_ref[qk_cols, :], hb, nt, preferred_element_type=F32)

        ext = jnp.concatenate([carry_ref[bb, :, qk_cols], qk_pre], axis=0)
        carry_ref[bb, :, qk_cols] = qk_pre[L - CONV_CARRY:, :]
        conv = ext * convw_ref[CONV_K - 1:CONV_K, qk_cols]
        for d in range(1, CONV_K):
            conv = conv + pltpu.roll(ext, d, 0) * convw_ref[CONV_K - 1 - d:CONV_K - d, qk_cols]
        qk = conv[CONV_CARRY:, :] + convb_ref[:, qk_cols]
        qk = qk * _sigmoid(qk)
        q2 = qk[:, :LANES].astype(BF16)
        k2 = (qk[:, LANES:] * (dqk ** -0.5)).astype(BF16)

        c_ext = c_ref[bb, p]
        k_stack = jnp.concatenate([jnp.where(first, k2, jnp.zeros_like(k2)),
                                   jnp.where(first, jnp.zeros_like(k2), k2)], axis=0)
        s_t = lax.dot_general(k_stack, q2, nt, preferred_element_type=F32)
        inter_ext = lax.dot_general(c_ext.astype(BF16), q2, nt, preferred_element_type=F32)
        scs = []
        for half, hh in enumerate(heads):
            p_t = jnp.exp(jnp.where(causal_t, u_t[hh:hh + 1, :] - r[:, hh:hh + 1], neg_inf))
            scs.append((s_t[half * L:(half + 1) * L, :] * p_t).astype(BF16))
        vb = v_t.astype(BF16)
        v_ext = jnp.concatenate([jnp.concatenate([vb[:dv], zeros_t], axis=1),
                                 jnp.concatenate([zeros_t, vb[dv:]], axis=1), ones_rows], axis=0)
        num_ext = jnp.dot(v_ext, jnp.concatenate(scs, axis=0), preferred_element_type=F32)
        outs = []
        for half, hh in enumerate(heads):
            hs = slice(half * dv, (half + 1) * dv)
            si = s_inter_t[hh:hh + 1, :]
            num = num_ext[hs] + si * inter_ext[hs]
            den = num_ext[pw + half:pw + half + 1] + si * inter_ext[pw + half:pw + half + 1]
            ho = num * (1.0 / jnp.maximum(jnp.abs(den), clamp_t[hh:hh + 1, :]))
            outs.append(ho * lax.rsqrt(jnp.mean(ho * ho, axis=0, keepdims=True) + EPS))
        hnorm = hnorm_ref[qk_cols, :]
        gated_t = (jnp.concatenate(outs, axis=0) * jnp.concatenate([hnorm] * (L // LANES), axis=1)
                   * _sigmoid(o_t))
        proj_t = jnp.dot(wout_t_ref[:, qk_cols], gated_t.astype(BF16), preferred_element_type=F32)
        acc_t = proj_t if acc_t is None else acc_t + proj_t

        ws_a, ws_b = (w_s_t[hh:hh + 1, :] for hh in heads)
        wsv_ext = jnp.concatenate([ws_a * v_t[:dv], ws_b * v_t[dv:], ws_a, ws_b, zeros_x], axis=0).astype(BF16)
        upd = jnp.dot(wsv_ext, k2, preferred_element_type=F32)
        d_a, d_b = (decay[hh:hh + 1, :] for hh in heads)
        c_ref[bb, p] = jnp.where(rows_a, d_a, d_b) * c_ext + jnp.where(own_lanes, upd, 0.0)
    o_ref[bb] = x + acc_t.T


def _pair_major(a, n_pairs):
    half = a.shape[-1] // 2
    blk = half // n_pairs
    parts = []
    for p in range(n_pairs):
        parts += [a[..., p * blk:(p + 1) * blk], a[..., half + p * blk:half + (p + 1) * blk]]
    return jnp.concatenate(parts, axis=-1)


def _mlstm_layer(h, g, w_in, conv_w, conv_b, b_i, b_f, head_norm, w_out):
    B, S, D = h.shape
    H = MLSTM_HEADS
    v_w = w_out.shape[0]
    qk2 = conv_w.shape[1]
    assert w_in.shape[1] == qk2 + 2 * v_w + 2 * H and qk2 == v_w
    L = min(MLSTM_CHUNK, S)
    nb = MLSTM_BATCH_PER_STEP
    assert S % L == 0 and B % nb == 0 and L % LANES == 0
    w_gate = jnp.concatenate(_split_bf16(w_in[:, qk2 + 2 * v_w:], 2), axis=1)
    const2 = lambda b, j: (0, 0)
    full = lambda a: pl.BlockSpec(a.shape, const2)
    operands = [g.reshape(1, D), _pair_major(w_in[:, :qk2], H // 2).astype(BF16),
                w_in[:, qk2:qk2 + v_w].T.astype(BF16), w_in[:, qk2 + v_w:qk2 + 2 * v_w].T.astype(BF16),
                w_gate.T, _pair_major(conv_w, H // 2), _pair_major(conv_b.reshape(1, qk2), H // 2),
                b_i.reshape(H, 1), b_f.reshape(H, 1),
                jnp.broadcast_to(head_norm.reshape(v_w, 1), (v_w, LANES)), w_out.T.astype(BF16)]
    return pl.pallas_call(
        _mlstm_kernel,
        grid=(B // nb, S // L),
        in_specs=[pl.BlockSpec((nb, L, D), lambda b, j: (b, j, 0))] + [full(a) for a in operands],
        out_specs=pl.BlockSpec((nb, L, D), lambda b, j: (b, j, 0)),
        out_shape=jax.ShapeDtypeStruct((B, S, D), F32),
        scratch_shapes=[
            pltpu.VMEM((nb, CONV_CARRY, qk2), F32),
            pltpu.VMEM((nb, H // 2, 2 * v_w // H + STATE_EXTRA_ROWS, LANES), F32),
            pltpu.VMEM((nb, H, 1), F32),
        ],
        compiler_params=pltpu.CompilerParams(
            dimension_semantics=("arbitrary", "arbitrary"), vmem_limit_bytes=VMEM_LIMIT_BYTES),
        name="mlstm_mixer",
    )(h, *operands)


def kernel(x, mix_norm, pool_w, pool_scale, w_in, conv_w, conv_b, gate_bias_i, gate_bias_f, head_norm,
           w_out, ffn_norm, w_group_router, b_group_router, w_expert_router, b_expert_router, w_gate,
           w_up, w_down, final_norm):
    B, S, D = x.shape
    depth = mix_norm.shape[0]
    n_mixers = 2
    expert_w = (w_gate, w_up, w_down)
    expert_w_b = None
    h = x
    for i in range(depth):
        j = i // n_mixers
        if i % n_mixers == 0:
            h, casted = _pool_layer(h, mix_norm[i], pool_w[j], pool_scale[j],
                                    (i, expert_w) if expert_w_b is None else None)
            expert_w_b = expert_w_b or casted
        else:
            h = _mlstm_layer(h, mix_norm[i], w_in[j], conv_w[j], conv_b[j], gate_bias_i[j],
                             gate_bias_f[j], head_norm[j], w_out[j])
            expert_w_b = expert_w_b or [w[i:i + 1].astype(BF16) for w in expert_w]
        fn = final_norm if i == depth - 1 else None
        cast_next = (i + 1, expert_w) if i + 1 < depth else None
        h, expert_w_b = _moe_layer(h.reshape(B * S, D), ffn_norm[i], w_group_router[i], b_group_router[i],
                                   w_expert_router[i], b_expert_router[i], *expert_w_b, fn, cast_next)
        h = h.reshape(B, S, D)
    return h
```

```python
import functools

import jax
import jax.numpy as jnp
from jax import lax
from jax.experimental import pallas as pl
from jax.experimental.pallas import tpu as pltpu

EPS = 1e-6
POOL_WINDOWS = (2, 4, 8, 16)
MLSTM_HEADS = 8
CONV_K = 4

SUBLANES = 8
LANES = 128
VMEM_LIMIT_BYTES = 56 * 1024 * 1024

POOL_SEQ_TILE = 512
POOL_HALO = 16
MOE_TOKEN_TILE = 1024
MOE_EXPERTS_PER_STEP = 4
ROUTER_LO_LANE = 32
MLSTM_CHUNK = 256
MLSTM_BATCH_PER_STEP = 4
CONV_CARRY = SUBLANES
STATE_EXTRA_ROWS = 16

F32 = jnp.float32
BF16 = jnp.bfloat16
_DONE = object()


def _rmsnorm(v, g):
    return (v * lax.rsqrt(jnp.mean(v * v, axis=-1, keepdims=True) + EPS)) * g


def _sigmoid(v):
    return 1.0 / (1.0 + jnp.exp(-v))


def _log_sigmoid(v):
    return jnp.minimum(v, 0.0) - jnp.log1p(jnp.exp(-jnp.abs(v)))


def _split_bf16(a, parts):
    out = []
    for _ in range(parts - 1):
        hi = a.astype(BF16)
        out.append(hi)
        a = a - hi.astype(F32)
    out.append(a.astype(BF16))
    return out


def _side_cast_specs(cast, grid):
    if cast is None:
        return [], [], [], []
    layer, arrays = cast
    n_steps = grid[0] * grid[1]
    in_specs, out_specs, out_shapes = [], [], []
    for w in arrays:
        n_e = w.shape[1]
        assert n_steps % n_e == 0
        slabs = n_steps // n_e
        rows = w.shape[2] // slabs
        assert rows * slabs == w.shape[2] and rows % (2 * SUBLANES) == 0
        blk = (1, 1, rows, w.shape[3])
        slab_of = lambda i, j, slabs=slabs: ((i * grid[1] + j) // slabs, (i * grid[1] + j) % slabs)
        in_specs.append(pl.BlockSpec(blk, lambda i, j, f=slab_of: (layer, *f(i, j), 0)))
        out_specs.append(pl.BlockSpec(blk, lambda i, j, f=slab_of: (0, *f(i, j), 0)))
        out_shapes.append(jax.ShapeDtypeStruct((1,) + w.shape[1:], BF16))
    return list(arrays), in_specs, out_specs, out_shapes


def _side_cast(srcs, dsts):
    for src, dst in zip(srcs, dsts):
        dst[...] = src[...].astype(BF16)


def _pool_kernel(x_ref, halo_ref, g_ref, w_ref, scale_ref, *rest):
    n_cast = len(rest) // 2
    o_ref = rest[n_cast]
    _side_cast(rest[:n_cast], rest[n_cast + 1:])
    j = pl.program_id(1)
    ts = x_ref.shape[1]
    gc = w_ref.shape[1]
    x = x_ref[0]
    g = g_ref[...]
    hn = _rmsnorm(x, g)
    hh = _rmsnorm(halo_ref[0], g) * (j > 0).astype(F32)
    s = jnp.concatenate([hh, hn], axis=0)
    sums = []
    shift = 1
    for _ in POOL_WINDOWS:
        s = s + pltpu.roll(s, shift, 0)
        sums.append(s[POOL_HALO:, :gc])
        s = s[:, gc:]
        shift *= 2
    t = (j * ts + lax.broadcasted_iota(jnp.int32, (ts, 1), 0) + 1).astype(F32)
    ys = []
    for k, w in enumerate(POOL_WINDOWS):
        count = jnp.minimum(t, float(w))
        pooled = sums[k] / count - hn[:, k * gc:(k + 1) * gc]
        ys.append(jnp.dot(pooled.astype(BF16), w_ref[k], preferred_element_type=F32))
    y = jnp.concatenate(ys, axis=-1)
    o_ref[0] = x + y * scale_ref[...]


def _pool_layer(x, g, w, scale, cast=None):
    B, S, D = x.shape
    ts = min(POOL_SEQ_TILE, S)
    assert S % ts == 0 and ts % POOL_HALO == 0
    n_groups, gc, _ = w.shape
    assert n_groups == len(POOL_WINDOWS) and n_groups * gc == D
    halo_blocks = ts // POOL_HALO
    cast_arrays, cast_in_specs, cast_out_specs, cast_out_shapes = _side_cast_specs(cast, (B, S // ts))
    out, *casted = pl.pallas_call(
        _pool_kernel,
        grid=(B, S // ts),
        in_specs=[
            pl.BlockSpec((1, ts, D), lambda b, j: (b, j, 0)),
            pl.BlockSpec((1, POOL_HALO, D), lambda b, j: (b, jnp.maximum(j * halo_blocks - 1, 0), 0)),
            pl.BlockSpec((1, D), lambda b, j: (0, 0)),
            pl.BlockSpec((n_groups, gc, gc), lambda b, j: (0, 0, 0)),
            pl.BlockSpec((1, D), lambda b, j: (0, 0)),
        ] + cast_in_specs,
        out_specs=[pl.BlockSpec((1, ts, D), lambda b, j: (b, j, 0))] + cast_out_specs,
        out_shape=[jax.ShapeDtypeStruct((B, S, D), F32)] + cast_out_shapes,
        compiler_params=pltpu.CompilerParams(
            dimension_semantics=("arbitrary", "arbitrary"), vmem_limit_bytes=VMEM_LIMIT_BYTES),
        name="pool_mixer",
    )(x, x, g.reshape(1, D), w.astype(BF16), scale.reshape(1, D), *cast_arrays)
    return out, casted


def _route_t(logits_t, bgr, ber):
    n_e, n_g = ber.shape[0], bgr.shape[0]
    el = logits_t[:n_e, :] + ber
    gl = logits_t[n_e:n_e + n_g, :] + bgr
    gidx = lax.broadcasted_iota(jnp.int32, gl.shape, 0)
    gmax = jnp.max(gl, axis=0, keepdims=True)
    gsel = jnp.min(jnp.where(gl == gmax, gidx, n_g), axis=0, keepdims=True)
    pg = 1.0 / jnp.sum(jnp.exp(gl - gmax), axis=0, keepdims=True)
    eidx = lax.broadcasted_iota(jnp.int32, el.shape, 0)
    in_group = (eidx // (n_e // n_g)) == gsel
    neg_inf = jnp.float32(-jnp.inf)
    cand = jnp.where(in_group, el, neg_inf)
    v1 = jnp.max(cand, axis=0, keepdims=True)
    i1 = jnp.min(jnp.where(cand == v1, eidx, n_e), axis=0, keepdims=True)
    cand2 = jnp.where(eidx == i1, neg_inf, cand)
    v2 = jnp.max(cand2, axis=0, keepdims=True)
    i2 = jnp.min(jnp.where(cand2 == v2, eidx, n_e), axis=0, keepdims=True)
    r = jnp.exp(v2 - v1)
    w1 = pg / (1.0 + r)
    w2 = pg * r / (1.0 + r)
    return jnp.where(eidx == i1, w1, 0.0) + jnp.where(eidx == i2, w2, 0.0)


def _moe_kernel(h_ref, g_ref, wrc_ref, wrh_ref, bgr_ref, ber_ref, wg_ref, wu_ref, wd_ref, fn_ref, *rest,
                final_norm, n_cast):
    cast_in, (o_ref, *cast_out), (hn_ref, comb_ref) = rest[:n_cast], rest[n_cast:2 * n_cast + 1], rest[2 * n_cast + 1:]
    _side_cast(cast_in, cast_out)
    step = pl.program_id(1)
    eps = wg_ref.shape[1]
    tm = h_ref.shape[0]
    n_f, n_d = wd_ref.shape[2:]

    def experts(hn, comb):
        eidx = lax.broadcasted_iota(jnp.int32, comb.shape, 1)
        acts = []
        for jj in range(eps):
            comb_e = jnp.sum(jnp.where(eidx == step * eps + jj, comb, 0.0), axis=-1, keepdims=True)
            hg = jnp.dot(hn, wg_ref[0, jj], preferred_element_type=F32)
            hu = jnp.dot(hn, wu_ref[0, jj], preferred_element_type=F32)
            acts.append(((hg * _sigmoid(hg)) * hu * comb_e).astype(BF16))
        return jnp.dot(jnp.concatenate(acts, axis=-1), wd_ref[0].reshape(eps * n_f, n_d),
                       preferred_element_type=F32)

    @pl.when(step == 0)
    def _():
        x = h_ref[...]
        hn, hn_lo = _split_bf16(_rmsnorm(x, g_ref[...]), 2)
        logits_t = (jnp.dot(hn, wrc_ref[...], preferred_element_type=F32)
                    + jnp.dot(hn_lo, wrh_ref[...], preferred_element_type=F32)).T
        n_l = ROUTER_LO_LANE
        comb_t = _route_t(logits_t[:n_l, :] + logits_t[n_l:2 * n_l, :], bgr_ref[...], ber_ref[...])
        comb_t = jnp.concatenate([comb_t, jnp.zeros((LANES - comb_t.shape[0], tm), F32)], axis=0)
        comb = comb_t.T
        comb_ref[...] = comb
        hn_ref[...] = hn
        o_ref[...] = x + experts(hn, comb)

    @pl.when(step > 0)
    def _():
        o_ref[...] += experts(hn_ref[...], comb_ref[...])

    if final_norm:
        @pl.when(step == pl.num_programs(1) - 1)
        def _():
            o_ref[...] = _rmsnorm(o_ref[...], fn_ref[...])


def _moe_layer(h, g, wgr, bgr, wer, ber, wg, wu, wd, fn=None, cast_next=None):
    T, D = h.shape
    _, n_e, _, F = wg.shape
    n_g = wgr.shape[-1]
    tm = min(MOE_TOKEN_TILE, T)
    eps = MOE_EXPERTS_PER_STEP
    assert T % tm == 0 and n_e % eps == 0 and n_e + n_g <= ROUTER_LO_LANE
    n_inner = n_e // eps
    final_norm = fn is not None
    if fn is None:
        fn = jnp.ones((D,), F32)
    cast_arrays, cast_in_specs, cast_out_specs, cast_out_shapes = _side_cast_specs(cast_next, (T // tm, n_inner))
    wr_hi, wr_lo = _split_bf16(jnp.concatenate([wer, wgr], axis=1), 2)
    pad = lambda a, n: jnp.pad(a, ((0, 0), (0, n - a.shape[1])))
    wr_cat = jnp.concatenate([pad(wr_hi, ROUTER_LO_LANE), pad(wr_lo, LANES - ROUTER_LO_LANE)], axis=1)
    const2 = lambda i, e: (0, 0)
    full = lambda a: pl.BlockSpec(a.shape, const2)
    small = [g.reshape(1, D), wr_cat, pad(wr_hi, LANES), bgr.reshape(n_g, 1), ber.reshape(n_e, 1)]
    out, *cast = pl.pallas_call(
        functools.partial(_moe_kernel, final_norm=final_norm, n_cast=len(cast_arrays)),
        grid=(T // tm, n_inner),
        in_specs=[pl.BlockSpec((tm, D), lambda i, e: (i, 0))] + [full(a) for a in small] + [
            pl.BlockSpec((1, eps, D, F), lambda i, e: (0, e, 0, 0)),
            pl.BlockSpec((1, eps, D, F), lambda i, e: (0, e, 0, 0)),
            pl.BlockSpec((1, eps, F, D), lambda i, e: (0, e, 0, 0)),
            pl.BlockSpec((1, D), const2),
        ] + cast_in_specs,
        out_specs=[pl.BlockSpec((tm, D), lambda i, e: (i, 0))] + cast_out_specs,
        out_shape=[jax.ShapeDtypeStruct((T, D), F32)] + cast_out_shapes,
        scratch_shapes=[pltpu.VMEM((tm, D), BF16), pltpu.VMEM((tm, LANES), F32)],
        compiler_params=pltpu.CompilerParams(
            dimension_semantics=("arbitrary", "arbitrary"), vmem_limit_bytes=VMEM_LIMIT_BYTES),
        name="moe_final" if final_norm else "moe",
    )(h, *small, wg, wu, wd, fn.reshape(1, D), *cast_arrays)
    return out, cast


def _mlstm_kernel(h_ref, g_ref, wqk_ref, wv_t_ref, wo_t_ref, wgate_t_ref, convw_ref, convb_ref,
                  bi_t_ref, bf_t_ref, hnorm_ref, wout_t_ref, o_ref,
                  carry_ref, c_ref, m_ref):
    @pl.when(pl.program_id(1) == 0)
    def _():
        carry_ref[...] = jnp.zeros_like(carry_ref)
        c_ref[...] = jnp.zeros_like(c_ref)
        m_ref[...] = jnp.zeros_like(m_ref)

    rows = [_mlstm_chunk(bb, h_ref, g_ref, wqk_ref, wv_t_ref, wo_t_ref, wgate_t_ref, convw_ref, convb_ref,
                         bi_t_ref, bf_t_ref, hnorm_ref, wout_t_ref, o_ref, carry_ref, c_ref, m_ref)
            for bb in range(h_ref.shape[0])]
    while rows:
        rows = [r for r in rows if next(r, _DONE) is not _DONE]


def _mlstm_chunk(bb, h_ref, g_ref, wqk_ref, wv_t_ref, wo_t_ref, wgate_t_ref, convw_ref, convb_ref,
                 bi_t_ref, bf_t_ref, hnorm_ref, wout_t_ref, o_ref, carry_ref, c_ref, m_ref):
    L = h_ref.shape[1]
    H = MLSTM_HEADS
    qk2 = convw_ref.shape[1]
    v_w = wout_t_ref.shape[1]
    dqk = qk2 // (2 * H)
    dv = v_w // H
    pw = 2 * dv
    ext_rows = c_ref.shape[2]
    assert 2 * dqk == LANES and dv == LANES and ext_rows == pw + STATE_EXTRA_ROWS
    nt = (((1,), (1,)), ((), ()))

    x = h_ref[bb]
    hn = _rmsnorm(x, g_ref[...])
    hb = hn.astype(BF16)
    gates_t = lax.dot_general(wgate_t_ref[...], hb, nt, preferred_element_type=F32)
    gates_t = gates_t[:2 * H, :] + gates_t[2 * H:, :]
    li_t = gates_t[:H, :] + bi_t_ref[...]
    lf_t = _log_sigmoid(gates_t[H:, :] + bf_t_ref[...])
    key = lax.broadcasted_iota(jnp.int32, (L, L), 0)
    qry = lax.broadcasted_iota(jnp.int32, (L, L), 1)
    causal_t = key <= qry
    b_t = sum(jnp.dot(part, causal_t.astype(BF16), preferred_element_type=F32)
              for part in _split_bf16(lf_t, 3))
    r_t = b_t - li_t
    r = jnp.concatenate([r_t, jnp.zeros((LANES - H, L), F32)], axis=0).T

    yield
    lane_t = lax.broadcasted_iota(jnp.int32, (1, L), 1)
    neg_inf = jnp.float32(-jnp.inf)
    best = li_t - b_t
    shift = 1
    while shift < L:
        best = jnp.maximum(best, jnp.where(lane_t >= shift, pltpu.roll(best, shift, 1), neg_inf))
        shift *= 2
    m_prev = m_ref[bb]
    inter_t = b_t + m_prev
    m_t = jnp.maximum(inter_t, b_t + best)
    s_inter_t = jnp.exp(inter_t - m_t)
    u_t = b_t - m_t
    clamp_t = jnp.exp(-m_t)
    b_last = b_t[:, L - 1:L]
    gsum_t = b_last - b_t + li_t
    m_new = jnp.maximum(b_last + m_prev, jnp.max(gsum_t, axis=1, keepdims=True))
    w_s_t = jnp.exp(gsum_t - m_new)
    decay = jnp.exp(b_last + m_prev - m_new)
    m_ref[bb] = m_new

    yield
    first = lax.broadcasted_iota(jnp.int32, (1, LANES), 1) < dqk
    erow = lax.broadcasted_iota(jnp.int32, (ext_rows, 1), 0)
    rows_a = (erow < dv) | (erow == pw)
    rows_b = ((erow >= dv) & (erow < pw)) | (erow == pw + 1)
    own_lanes = (rows_a & first) | (rows_b & jnp.logical_not(first))
    xrow = lax.broadcasted_iota(jnp.int32, (STATE_EXTRA_ROWS, 2 * L), 0)
    xcol = lax.broadcasted_iota(jnp.int32, (STATE_EXTRA_ROWS, 2 * L), 1)
    ones_rows = (((xrow == 0) & (xcol < L)) | ((xrow == 1) & (xcol >= L))).astype(BF16)
    zeros_t = jnp.zeros((dv, L), BF16)
    zeros_x = jnp.zeros((STATE_EXTRA_ROWS - 2, L), F32)
    projs = []

    def pair(p):
        heads = (2 * p, 2 * p + 1)
        qk_cols = slice(p * pw, (p + 1) * pw)
        qk_pre = jnp.dot(hb, wqk_ref[:, qk_cols], preferred_element_type=F32)
        v_t = lax.dot_general(wv_t_ref[qk_cols, :], hb, nt, preferred_element_type=F32)
        o_t = lax.dot_general(wo_t_ref[qk_cols, :], hb, nt, preferred_element_type=F32)

        yield
        ext = jnp.concatenate([carry_ref[bb, :, qk_cols], qk_pre], axis=0)
        carry_ref[bb, :, qk_cols] = qk_pre[L - CONV_CARRY:, :]
        conv = ext * convw_ref[CONV_K - 1:CONV_K, qk_cols]
        for d in range(1, CONV_K):
            conv = conv + pltpu.roll(ext, d, 0) * convw_ref[CONV_K - 1 - d:CONV_K - d, qk_cols]
        qk = conv[CONV_CARRY:, :] + convb_ref[:, qk_cols]
        qk = qk * _sigmoid(qk)
        q2 = qk[:, :LANES].astype(BF16)
        k2 = (qk[:, LANES:] * (dqk ** -0.5)).astype(BF16)

        yield
        c_ext = c_ref[bb, p]
        k_stack = jnp.concatenate([jnp.where(first, k2, jnp.zeros_like(k2)),
                                   jnp.where(first, jnp.zeros_like(k2), k2)], axis=0)
        s_t = lax.dot_general(k_stack, q2, nt, preferred_element_type=F32)
        inter_ext = lax.dot_general(c_ext.astype(BF16), q2, nt, preferred_element_type=F32)
        yield
        scs = []
        for half, hh in enumerate(heads):
            p_t = jnp.exp(jnp.where(causal_t, u_t[hh:hh + 1, :] - r[:, hh:hh + 1], neg_inf))
            scs.append((s_t[half * L:(half + 1) * L, :] * p_t).astype(BF16))
            yield
        vb = v_t.astype(BF16)
        v_ext = jnp.concatenate([jnp.concatenate([vb[:dv], zeros_t], axis=1),
                                 jnp.concatenate([zeros_t, vb[dv:]], axis=1), ones_rows], axis=0)
        num_ext = jnp.dot(v_ext, jnp.concatenate(scs, axis=0), preferred_element_type=F32)
        yield
        outs = []
        for half, hh in enumerate(heads):
            hs = slice(half * dv, (half + 1) * dv)
            si = s_inter_t[hh:hh + 1, :]
            num = num_ext[hs] + si * inter_ext[hs]
            den = num_ext[pw + half:pw + half + 1] + si * inter_ext[pw + half:pw + half + 1]
            ho = num * (1.0 / jnp.maximum(jnp.abs(den), clamp_t[hh:hh + 1, :]))
            outs.append(ho * lax.rsqrt(jnp.mean(ho * ho, axis=0, keepdims=True) + EPS))
            yield
        hnorm = hnorm_ref[qk_cols, :]
        gated_t = (jnp.concatenate(outs, axis=0) * jnp.concatenate([hnorm] * (L // LANES), axis=1)
                   * _sigmoid(o_t))
        projs.append(jnp.dot(wout_t_ref[:, qk_cols], gated_t.astype(BF16), preferred_element_type=F32))

        yield
        ws_a, ws_b = (w_s_t[hh:hh + 1, :] for hh in heads)
        wsv_ext = jnp.concatenate([ws_a * v_t[:dv], ws_b * v_t[dv:], ws_a, ws_b, zeros_x], axis=0).astype(BF16)
        upd = jnp.dot(wsv_ext, k2, preferred_element_type=F32)
        d_a, d_b = (decay[hh:hh + 1, :] for hh in heads)
        c_ref[bb, p] = jnp.where(rows_a, d_a, d_b) * c_ext + jnp.where(own_lanes, upd, 0.0)

    pairs = [pair(p) for p in range(H // 2)]
    while pairs:
        pairs = [g for g in pairs if next(g, _DONE) is not _DONE]
        yield
    o_ref[bb] = x + sum(projs).T


def _pair_major(a, n_pairs):
    half = a.shape[-1] // 2
    blk = half // n_pairs
    parts = []
    for p in range(n_pairs):
        parts += [a[..., p * blk:(p + 1) * blk], a[..., half + p * blk:half + (p + 1) * blk]]
    return jnp.concatenate(parts, axis=-1)


def _mlstm_layer(h, g, w_in, conv_w, conv_b, b_i, b_f, head_norm, w_out):
    B, S, D = h.shape
    H = MLSTM_HEADS
    v_w = w_out.shape[0]
    qk2 = conv_w.shape[1]
    assert w_in.shape[1] == qk2 + 2 * v_w + 2 * H and qk2 == v_w
    L = min(MLSTM_CHUNK, S)
    nb = MLSTM_BATCH_PER_STEP
    assert S % L == 0 and B % nb == 0 and L % LANES == 0
    w_gate = jnp.concatenate(_split_bf16(w_in[:, qk2 + 2 * v_w:], 2), axis=1)
    const2 = lambda b, j: (0, 0)
    full = lambda a: pl.BlockSpec(a.shape, const2)
    operands = [g.reshape(1, D), _pair_major(w_in[:, :qk2], H // 2).astype(BF16),
                w_in[:, qk2:qk2 + v_w].T.astype(BF16), w_in[:, qk2 + v_w:qk2 + 2 * v_w].T.astype(BF16),
                w_gate.T, _pair_major(conv_w, H // 2), _pair_major(conv_b.reshape(1, qk2), H // 2),
                b_i.reshape(H, 1), b_f.reshape(H, 1),
                jnp.broadcast_to(head_norm.reshape(v_w, 1), (v_w, LANES)), w_out.T.astype(BF16)]
    return pl.pallas_call(
        _mlstm_kernel,
        grid=(B // nb, S // L),
        in_specs=[pl.BlockSpec((nb, L, D), lambda b, j: (b, j, 0))] + [full(a) for a in operands],
        out_specs=pl.BlockSpec((nb, L, D), lambda b, j: (b, j, 0)),
        out_shape=jax.ShapeDtypeStruct((B, S, D), F32),
        scratch_shapes=[
            pltpu.VMEM((nb, CONV_CARRY, qk2), F32),
            pltpu.VMEM((nb, H // 2, 2 * v_w // H + STATE_EXTRA_ROWS, LANES), F32),
            pltpu.VMEM((nb, H, 1), F32),
        ],
        compiler_params=pltpu.CompilerParams(
            dimension_semantics=("arbitrary", "arbitrary"), vmem_limit_bytes=VMEM_LIMIT_BYTES),
        name="mlstm_mixer",
    )(h, *operands)


def kernel(x, mix_norm, pool_w, pool_scale, w_in, conv_w, conv_b, gate_bias_i, gate_bias_f, head_norm,
           w_out, ffn_norm, w_group_router, b_group_router, w_expert_router, b_expert_router, w_gate,
           w_up, w_down, final_norm):
    B, S, D = x.shape
    depth = mix_norm.shape[0]
    n_mixers = 2
    expert_w = (w_gate, w_up, w_down)
    expert_w_b = None
    h = x
    for i in range(depth):
        j = i // n_mixers
        if i % n_mixers == 0:
            h, casted = _pool_layer(h, mix_norm[i], pool_w[j], pool_scale[j],
                                    (i, expert_w) if expert_w_b is None else None)
            expert_w_b = expert_w_b or casted
        else:
            h = _mlstm_layer(h, mix_norm[i], w_in[j], conv_w[j], conv_b[j], gate_bias_i[j],
                             gate_bias_f[j], head_norm[j], w_out[j])
            expert_w_b = expert_w_b or [w[i:i + 1].astype(BF16) for w in expert_w]
        fn = final_norm if i == depth - 1 else None
        cast_next = (i + 1, expert_w) if i + 1 < depth else None
        h, expert_w_b = _moe_layer(h.reshape(B * S, D), ffn_norm[i], w_group_router[i], b_group_router[i],
                                   w_expert_router[i], b_expert_router[i], *expert_w_b, fn, cast_next)
        h = h.reshape(B, S, D)
    return h
```

```python
import functools

import jax
import jax.numpy as jnp
from jax import lax
from jax.experimental import pallas as pl
from jax.experimental.pallas import tpu as pltpu

EPS = 1e-6
POOL_WINDOWS = (2, 4, 8, 16)
MLSTM_HEADS = 8
CONV_K = 4

SUBLANES = 8
LANES = 128
VMEM_LIMIT_BYTES = 56 * 1024 * 1024

POOL_SEQ_TILE = 512
POOL_HALO = 16
MOE_TOKEN_TILE = 1024
MOE_EXPERTS_PER_STEP = 4
ROUTER_LO_LANE = 32
MLSTM_CHUNK = 256
MLSTM_BATCH_PER_STEP = 4
CONV_CARRY = SUBLANES
STATE_EXTRA_ROWS = 16

F32 = jnp.float32
BF16 = jnp.bfloat16
_DONE = object()


def _rmsnorm(v, g):
    return (v * lax.rsqrt(jnp.mean(v * v, axis=-1, keepdims=True) + EPS)) * g


def _sigmoid(v):
    return 1.0 / (1.0 + jnp.exp(-v))


def _log_sigmoid(v):
    return jnp.minimum(v, 0.0) - jnp.log1p(jnp.exp(-jnp.abs(v)))


def _split_bf16(a, parts):
    out = []
    for _ in range(parts - 1):
        hi = a.astype(BF16)
        out.append(hi)
        a = a - hi.astype(F32)
    out.append(a.astype(BF16))
    return out


def _side_cast_specs(cast, grid):
    if cast is None:
        return [], [], [], []
    layer, arrays = cast
    n_steps = grid[0] * grid[1]
    in_specs, out_specs, out_shapes = [], [], []
    for w in arrays:
        n_e = w.shape[1]
        assert n_steps % n_e == 0
        slabs = n_steps // n_e
        rows = w.shape[2] // slabs
        assert rows * slabs == w.shape[2] and rows % (2 * SUBLANES) == 0
        blk = (1, 1, rows, w.shape[3])
        slab_of = lambda i, j, slabs=slabs: ((i * grid[1] + j) // slabs, (i * grid[1] + j) % slabs)
        in_specs.append(pl.BlockSpec(blk, lambda i, j, f=slab_of: (layer, *f(i, j), 0)))
        out_specs.append(pl.BlockSpec(blk, lambda i, j, f=slab_of: (0, *f(i, j), 0)))
        out_shapes.append(jax.ShapeDtypeStruct((1,) + w.shape[1:], BF16))
    return list(arrays), in_specs, out_specs, out_shapes


def _side_cast(srcs, dsts):
    for src, dst in zip(srcs, dsts):
        dst[...] = src[...].astype(BF16)


def _pool_kernel(x_ref, halo_ref, g_ref, w_ref, scale_ref, *rest):
    n_cast = len(rest) // 2
    o_ref = rest[n_cast]
    _side_cast(rest[:n_cast], rest[n_cast + 1:])
    j = pl.program_id(1)
    ts = x_ref.shape[1]
    gc = w_ref.shape[1]
    x = x_ref[0]
    g = g_ref[...]
    hn = _rmsnorm(x, g)
    hh = _rmsnorm(halo_ref[0], g) * (j > 0).astype(F32)
    s = jnp.concatenate([hh, hn], axis=0)
    sums = []
    shift = 1
    for _ in POOL_WINDOWS:
        s = s + pltpu.roll(s, shift, 0)
        sums.append(s[POOL_HALO:, :gc])
        s = s[:, gc:]
        shift *= 2
    t = (j * ts + lax.broadcasted_iota(jnp.int32, (ts, 1), 0) + 1).astype(F32)
    ys = []
    for k, w in enumerate(POOL_WINDOWS):
        count = jnp.minimum(t, float(w))
        pooled = sums[k] / count - hn[:, k * gc:(k + 1) * gc]
        ys.append(jnp.dot(pooled.astype(BF16), w_ref[k], preferred_element_type=F32))
    y = jnp.concatenate(ys, axis=-1)
    o_ref[0] = x + y * scale_ref[...]


def _pool_layer(x, g, w, scale, cast=None):
    B, S, D = x.shape
    ts = min(POOL_SEQ_TILE, S)
    assert S % ts == 0 and ts % POOL_HALO == 0
    n_groups, gc, _ = w.shape
    assert n_groups == len(POOL_WINDOWS) and n_groups * gc == D
    halo_blocks = ts // POOL_HALO
    cast_arrays, cast_in_specs, cast_out_specs, cast_out_shapes = _side_cast_specs(cast, (B, S // ts))
    out, *casted = pl.pallas_call(
        _pool_kernel,
        grid=(B, S // ts),
        in_specs=[
            pl.BlockSpec((1, ts, D), lambda b, j: (b, j, 0)),
            pl.BlockSpec((1, POOL_HALO, D), lambda b, j: (b, jnp.maximum(j * halo_blocks - 1, 0), 0)),
            pl.BlockSpec((1, D), lambda b, j: (0, 0)),
            pl.BlockSpec((n_groups, gc, gc), lambda b, j: (0, 0, 0)),
            pl.BlockSpec((1, D), lambda b, j: (0, 0)),
        ] + cast_in_specs,
        out_specs=[pl.BlockSpec((1, ts, D), lambda b, j: (b, j, 0))] + cast_out_specs,
        out_shape=[jax.ShapeDtypeStruct((B, S, D), F32)] + cast_out_shapes,
        compiler_params=pltpu.CompilerParams(
            dimension_semantics=("arbitrary", "arbitrary"), vmem_limit_bytes=VMEM_LIMIT_BYTES),
        name="pool_mixer",
    )(x, x, g.reshape(1, D), w.astype(BF16), scale.reshape(1, D), *cast_arrays)
    return out, casted


def _route_t(logits_t, bgr, ber):
    n_e, n_g = ber.shape[0], bgr.shape[0]
    el = logits_t[:n_e, :] + ber
    gl = logits_t[n_e:n_e + n_g, :] + bgr
    gidx = lax.broadcasted_iota(jnp.int32, gl.shape, 0)
    gmax = jnp.max(gl, axis=0, keepdims=True)
    gsel = jnp.min(jnp.where(gl == gmax, gidx, n_g), axis=0, keepdims=True)
    pg = 1.0 / jnp.sum(jnp.exp(gl - gmax), axis=0, keepdims=True)
    eidx = lax.broadcasted_iota(jnp.int32, el.shape, 0)
    in_group = (eidx // (n_e // n_g)) == gsel
    neg_inf = jnp.float32(-jnp.inf)
    cand = jnp.where(in_group, el, neg_inf)
    v1 = jnp.max(cand, axis=0, keepdims=True)
    i1 = jnp.min(jnp.where(cand == v1, eidx, n_e), axis=0, keepdims=True)
    cand2 = jnp.where(eidx == i1, neg_inf, cand)
    v2 = jnp.max(cand2, axis=0, keepdims=True)
    i2 = jnp.min(jnp.where(cand2 == v2, eidx, n_e), axis=0, keepdims=True)
    r = jnp.exp(v2 - v1)
    w1 = pg / (1.0 + r)
    w2 = pg * r / (1.0 + r)
    return jnp.where(eidx == i1, w1, 0.0) + jnp.where(eidx == i2, w2, 0.0)


def _moe_kernel(h_ref, g_ref, wrc_ref, wrh_ref, bgr_ref, ber_ref, wg_ref, wu_ref, wd_ref, fn_ref, *rest,
                final_norm, n_cast):
    cast_in, (o_ref, *cast_out), (hn_ref, comb_ref) = rest[:n_cast], rest[n_cast:2 * n_cast + 1], rest[2 * n_cast + 1:]
    _side_cast(cast_in, cast_out)
    step = pl.program_id(1)
    eps = wg_ref.shape[1]
    tm = h_ref.shape[0]
    n_f, n_d = wd_ref.shape[2:]

    def experts(hn, comb):
        eidx = lax.broadcasted_iota(jnp.int32, comb.shape, 1)
        acts = []
        for jj in range(eps):
            comb_e = jnp.sum(jnp.where(eidx == step * eps + jj, comb, 0.0), axis=-1, keepdims=True)
            hg = jnp.dot(hn, wg_ref[0, jj], preferred_element_type=F32)
            hu = jnp.dot(hn, wu_ref[0, jj], preferred_element_type=F32)
            acts.append(((hg * _sigmoid(hg)) * hu * comb_e).astype(BF16))
        return jnp.dot(jnp.concatenate(acts, axis=-1), wd_ref[0].reshape(eps * n_f, n_d),
                       preferred_element_type=F32)

    @pl.when(step == 0)
    def _():
        x = h_ref[...]
        hn, hn_lo = _split_bf16(_rmsnorm(x, g_ref[...]), 2)
        logits_t = (jnp.dot(hn, wrc_ref[...], preferred_element_type=F32)
                    + jnp.dot(hn_lo, wrh_ref[...], preferred_element_type=F32)).T
        n_l = ROUTER_LO_LANE
        comb_t = _route_t(logits_t[:n_l, :] + logits_t[n_l:2 * n_l, :], bgr_ref[...], ber_ref[...])
        comb_t = jnp.concatenate([comb_t, jnp.zeros((LANES - comb_t.shape[0], tm), F32)], axis=0)
        comb = comb_t.T
        comb_ref[...] = comb
        hn_ref[...] = hn
        o_ref[...] = x + experts(hn, comb)

    @pl.when(step > 0)
    def _():
        o_ref[...] += experts(hn_ref[...], comb_ref[...])

    if final_norm:
        @pl.when(step == pl.num_programs(1) - 1)
        def _():
            o_ref[...] = _rmsnorm(o_ref[...], fn_ref[...])


def _moe_layer(h, g, wgr, bgr, wer, ber, wg, wu, wd, fn=None, cast_next=None):
    T, D = h.shape
    _, n_e, _, F = wg.shape
    n_g = wgr.shape[-1]
    tm = min(MOE_TOKEN_TILE, T)
    eps = MOE_EXPERTS_PER_STEP
    assert T % tm == 0 and n_e % eps == 0 and n_e + n_g <= ROUTER_LO_LANE
    n_inner = n_e // eps
    final_norm = fn is not None
    if fn is None:
        fn = jnp.ones((D,), F32)
    cast_arrays, cast_in_specs, cast_out_specs, cast_out_shapes = _side_cast_specs(cast_next, (T // tm, n_inner))
    wr_hi, wr_lo = _split_bf16(jnp.concatenate([wer, wgr], axis=1), 2)
    pad = lambda a, n: jnp.pad(a, ((0, 0), (0, n - a.shape[1])))
    wr_cat = jnp.concatenate([pad(wr_hi, ROUTER_LO_LANE), pad(wr_lo, LANES - ROUTER_LO_LANE)], axis=1)
    const2 = lambda i, e: (0, 0)
    full = lambda a: pl.BlockSpec(a.shape, const2)
    small = [g.reshape(1, D), wr_cat, pad(wr_hi, LANES), bgr.reshape(n_g, 1), ber.reshape(n_e, 1)]
    out, *cast = pl.pallas_call(
        functools.partial(_moe_kernel, final_norm=final_norm, n_cast=len(cast_arrays)),
        grid=(T // tm, n_inner),
        in_specs=[pl.BlockSpec((tm, D), lambda i, e: (i, 0))] + [full(a) for a in small] + [
            pl.BlockSpec((1, eps, D, F), lambda i, e: (0, e, 0, 0)),
            pl.BlockSpec((1, eps, D, F), lambda i, e: (0, e, 0, 0)),
            pl.BlockSpec((1, eps, F, D), lambda i, e: (0, e, 0, 0)),
            pl.BlockSpec((1, D), const2),
        ] + cast_in_specs,
        out_specs=[pl.BlockSpec((tm, D), lambda i, e: (i, 0))] + cast_out_specs,
        out_shape=[jax.ShapeDtypeStruct((T, D), F32)] + cast_out_shapes,
        scratch_shapes=[pltpu.VMEM((tm, D), BF16), pltpu.VMEM((tm, LANES), F32)],
        compiler_params=pltpu.CompilerParams(
            dimension_semantics=("arbitrary", "arbitrary"), vmem_limit_bytes=VMEM_LIMIT_BYTES),
        name="moe_final" if final_norm else "moe",
    )(h, *small, wg, wu, wd, fn.reshape(1, D), *cast_arrays)
    return out, cast


def _mlstm_kernel(h_ref, g_ref, wqk_ref, wv_t_ref, wo_t_ref, wgate_t_ref, convw_ref, convb_ref,
                  bi_t_ref, bf_t_ref, hnorm_ref, wout_t_ref, o_ref,
                  carry_ref, c_ref, m_ref):
    @pl.when(pl.program_id(1) == 0)
    def _():
        carry_ref[...] = jnp.zeros_like(carry_ref)
        c_ref[...] = jnp.zeros_like(c_ref)
        m_ref[...] = jnp.zeros_like(m_ref)

    rows = [_mlstm_chunk(bb, h_ref, g_ref, wqk_ref, wv_t_ref, wo_t_ref, wgate_t_ref, convw_ref, convb_ref,
                         bi_t_ref, bf_t_ref, hnorm_ref, wout_t_ref, o_ref, carry_ref, c_ref, m_ref)
            for bb in range(h_ref.shape[0])]
    while rows:
        rows = [r for r in rows if next(r, _DONE) is not _DONE]


def _mlstm_chunk(bb, h_ref, g_ref, wqk_ref, wv_t_ref, wo_t_ref, wgate_t_ref, convw_ref, convb_ref,
                 bi_t_ref, bf_t_ref, hnorm_ref, wout_t_ref, o_ref, carry_ref, c_ref, m_ref):
    L = h_ref.shape[1]
    H = MLSTM_HEADS
    qk2 = convw_ref.shape[1]
    v_w = wout_t_ref.shape[1]
    dqk = qk2 // (2 * H)
    dv = v_w // H
    pw = 2 * dv
    ext_rows = c_ref.shape[2]
    assert 2 * dqk == LANES and dv == LANES and ext_rows == pw + STATE_EXTRA_ROWS
    nt = (((1,), (1,)), ((), ()))

    x = h_ref[bb]
    hn = _rmsnorm(x, g_ref[...])
    hb = hn.astype(BF16)
    yield
    gates_t = lax.dot_general(wgate_t_ref[...], hb, nt, preferred_element_type=F32)
    gates_t = gates_t[:2 * H, :] + gates_t[2 * H:, :]
    li_t = gates_t[:H, :] + bi_t_ref[...]
    lf_t = _log_sigmoid(gates_t[H:, :] + bf_t_ref[...])
    key = lax.broadcasted_iota(jnp.int32, (L, L), 0)
    qry = lax.broadcasted_iota(jnp.int32, (L, L), 1)
    causal_t = key <= qry
    b_t = sum(jnp.dot(part, causal_t.astype(BF16), preferred_element_type=F32)
              for part in _split_bf16(lf_t, 3))
    r_t = b_t - li_t
    r = jnp.concatenate([r_t, jnp.zeros((LANES - H, L), F32)], axis=0).T

    yield
    lane_t = lax.broadcasted_iota(jnp.int32, (1, L), 1)
    neg_inf = jnp.float32(-jnp.inf)
    best = li_t - b_t
    shift = 1
    while shift < L:
        best = jnp.maximum(best, jnp.where(lane_t >= shift, pltpu.roll(best, shift, 1), neg_inf))
        shift *= 2
    m_prev = m_ref[bb]
    inter_t = b_t + m_prev
    m_t = jnp.maximum(inter_t, b_t + best)
    s_inter_t = jnp.exp(inter_t - m_t)
    u_t = b_t - m_t
    clamp_t = jnp.exp(-m_t)
    b_last = b_t[:, L - 1:L]
    gsum_t = b_last - b_t + li_t
    m_new = jnp.maximum(b_last + m_prev, jnp.max(gsum_t, axis=1, keepdims=True))
    w_s_t = jnp.exp(gsum_t - m_new)
    decay = jnp.exp(b_last + m_prev - m_new)
    m_ref[bb] = m_new

    yield
    first = lax.broadcasted_iota(jnp.int32, (1, LANES), 1) < dqk
    erow = lax.broadcasted_iota(jnp.int32, (ext_rows, 1), 0)
    rows_a = (erow < dv) | (erow == pw)
    rows_b = ((erow >= dv) & (erow < pw)) | (erow == pw + 1)
    own_lanes = (rows_a & first) | (rows_b & jnp.logical_not(first))
    xrow = lax.broadcasted_iota(jnp.int32, (STATE_EXTRA_ROWS, 2 * L), 0)
    xcol = lax.broadcasted_iota(jnp.int32, (STATE_EXTRA_ROWS, 2 * L), 1)
    ones_rows = (((xrow == 0) & (xcol < L)) | ((xrow == 1) & (xcol >= L))).astype(BF16)
    zeros_t = jnp.zeros((dv, L), BF16)
    zeros_x = jnp.zeros((STATE_EXTRA_ROWS - 2, L), F32)
    projs = []

    def pair(p):
        heads = (2 * p, 2 * p + 1)
        qk_cols = slice(p * pw, (p + 1) * pw)
        qk_pre = jnp.dot(hb, wqk_ref[:, qk_cols], preferred_element_type=F32)
        yield
        v_t = lax.dot_general(wv_t_ref[qk_cols, :], hb, nt, preferred_element_type=F32)
        yield
        o_t = lax.dot_general(wo_t_ref[qk_cols, :], hb, nt, preferred_element_type=F32)

        yield
        ext = jnp.concatenate([carry_ref[bb, :, qk_cols], qk_pre], axis=0)
        carry_ref[bb, :, qk_cols] = qk_pre[L - CONV_CARRY:, :]
        conv = ext * convw_ref[CONV_K - 1:CONV_K, qk_cols]
        for d in range(1, CONV_K):
            conv = conv + pltpu.roll(ext, d, 0) * convw_ref[CONV_K - 1 - d:CONV_K - d, qk_cols]
        qk = conv[CONV_CARRY:, :] + convb_ref[:, qk_cols]
        qk = qk * _sigmoid(qk)
        q2 = qk[:, :LANES].astype(BF16)
        k2 = (qk[:, LANES:] * (dqk ** -0.5)).astype(BF16)

        yield
        c_ext = c_ref[bb, p]
        k_stack = jnp.concatenate([jnp.where(first, k2, jnp.zeros_like(k2)),
                                   jnp.where(first, jnp.zeros_like(k2), k2)], axis=0)
        s_t = lax.dot_general(k_stack, q2, nt, preferred_element_type=F32)
        yield
        inter_ext = lax.dot_general(c_ext.astype(BF16), q2, nt, preferred_element_type=F32)
        yield
        scs = []
        for half, hh in enumerate(heads):
            p_t = jnp.exp(jnp.where(causal_t, u_t[hh:hh + 1, :] - r[:, hh:hh + 1], neg_inf))
            scs.append((s_t[half * L:(half + 1) * L, :] * p_t).astype(BF16))
            yield
        vb = v_t.astype(BF16)
        v_ext = jnp.concatenate([jnp.concatenate([vb[:dv], zeros_t], axis=1),
                                 jnp.concatenate([zeros_t, vb[dv:]], axis=1), ones_rows], axis=0)
        num_ext = jnp.dot(v_ext, jnp.concatenate(scs, axis=0), preferred_element_type=F32)
        yield
        outs = []
        for half, hh in enumerate(heads):
            hs = slice(half * dv, (half + 1) * dv)
            si = s_inter_t[hh:hh + 1, :]
            num = num_ext[hs] + si * inter_ext[hs]
            den = num_ext[pw + half:pw + half + 1] + si * inter_ext[pw + half:pw + half + 1]
            ho = num * (1.0 / jnp.maximum(jnp.abs(den), clamp_t[hh:hh + 1, :]))
            outs.append(ho * lax.rsqrt(jnp.mean(ho * ho, axis=0, keepdims=True) + EPS))
            yield
        hnorm = hnorm_ref[qk_cols, :]
        gated_t = (jnp.concatenate(outs, axis=0) * jnp.concatenate([hnorm] * (L // LANES), axis=1)
                   * _sigmoid(o_t))
        projs.append(jnp.dot(wout_t_ref[:, qk_cols], gated_t.astype(BF16), preferred_element_type=F32))

        yield
        ws_a, ws_b = (w_s_t[hh:hh + 1, :] for hh in heads)
        wsv_ext = jnp.concatenate([ws_a * v_t[:dv], ws_b * v_t[dv:], ws_a, ws_b, zeros_x], axis=0).astype(BF16)
        upd = jnp.dot(wsv_ext, k2, preferred_element_type=F32)
        d_a, d_b = (decay[hh:hh + 1, :] for hh in heads)
        c_ref[bb, p] = jnp.where(rows_a, d_a, d_b) * c_ext + jnp.where(own_lanes, upd, 0.0)

    pairs = [pair(p) for p in range(H // 2)]
    while pairs:
        pairs = [g for g in pairs if next(g, _DONE) is not _DONE]
        yield
    o_ref[bb] = x + sum(projs).T


def _pair_major(a, n_pairs):
    half = a.shape[-1] // 2
    blk = half // n_pairs
    parts = []
    for p in range(n_pairs):
        parts += [a[..., p * blk:(p + 1) * blk], a[..., half + p * blk:half + (p + 1) * blk]]
    return jnp.concatenate(parts, axis=-1)


def _mlstm_layer(h, g, w_in, conv_w, conv_b, b_i, b_f, head_norm, w_out):
    B, S, D = h.shape
    H = MLSTM_HEADS
    v_w = w_out.shape[0]
    qk2 = conv_w.shape[1]
    assert w_in.shape[1] == qk2 + 2 * v_w + 2 * H and qk2 == v_w
    L = min(MLSTM_CHUNK, S)
    nb = MLSTM_BATCH_PER_STEP
    assert S % L == 0 and B % nb == 0 and L % LANES == 0
    w_gate = jnp.concatenate(_split_bf16(w_in[:, qk2 + 2 * v_w:], 2), axis=1)
    const2 = lambda b, j: (0, 0)
    full = lambda a: pl.BlockSpec(a.shape, const2)
    operands = [g.reshape(1, D), _pair_major(w_in[:, :qk2], H // 2).astype(BF16),
                w_in[:, qk2:qk2 + v_w].T.astype(BF16), w_in[:, qk2 + v_w:qk2 + 2 * v_w].T.astype(BF16),
                w_gate.T, _pair_major(conv_w, H // 2), _pair_major(conv_b.reshape(1, qk2), H // 2),
                b_i.reshape(H, 1), b_f.reshape(H, 1),
                jnp.broadcast_to(head_norm.reshape(v_w, 1), (v_w, LANES)), w_out.T.astype(BF16)]
    return pl.pallas_call(
        _mlstm_kernel,
        grid=(B // nb, S // L),
        in_specs=[pl.BlockSpec((nb, L, D), lambda b, j: (b, j, 0))] + [full(a) for a in operands],
        out_specs=pl.BlockSpec((nb, L, D), lambda b, j: (b, j, 0)),
        out_shape=jax.ShapeDtypeStruct((B, S, D), F32),
        scratch_shapes=[
            pltpu.VMEM((nb, CONV_CARRY, qk2), F32),
            pltpu.VMEM((nb, H // 2, 2 * v_w // H + STATE_EXTRA_ROWS, LANES), F32),
            pltpu.VMEM((nb, H, 1), F32),
        ],
        compiler_params=pltpu.CompilerParams(
            dimension_semantics=("arbitrary", "arbitrary"), vmem_limit_bytes=VMEM_LIMIT_BYTES),
        name="mlstm_mixer",
    )(h, *operands)


def kernel(x, mix_norm, pool_w, pool_scale, w_in, conv_w, conv_b, gate_bias_i, gate_bias_f, head_norm,
           w_out, ffn_norm, w_group_router, b_group_router, w_expert_router, b_expert_router, w_gate,
           w_up, w_down, final_norm):
    B, S, D = x.shape
    depth = mix_norm.shape[0]
    n_mixers = 2
    expert_w = (w_gate, w_up, w_down)
    expert_w_b = None
    h = x
    for i in range(depth):
        j = i // n_mixers
        if i % n_mixers == 0:
            h, casted = _pool_layer(h, mix_norm[i], pool_w[j], pool_scale[j],
                                    (i, expert_w) if expert_w_b is None else None)
            expert_w_b = expert_w_b or casted
        else:
            h = _mlstm_layer(h, mix_norm[i], w_in[j], conv_w[j], conv_b[j], gate_bias_i[j],
                             gate_bias_f[j], head_norm[j], w_out[j])
            expert_w_b = expert_w_b or [w[i:i + 1].astype(BF16) for w in expert_w]
        fn = final_norm if i == depth - 1 else None
        cast_next = (i + 1, expert_w) if i + 1 < depth else None
        h, expert_w_b = _moe_layer(h.reshape(B * S, D), ffn_norm[i], w_group_router[i], b_group_router[i],
                                   w_expert_router[i], b_expert_router[i], *expert_w_b, fn, cast_next)
        h = h.reshape(B, S, D)
    return h
```

```python
import functools

import jax
import jax.numpy as jnp
from jax import lax
from jax.experimental import pallas as pl
from jax.experimental.pallas import tpu as pltpu

EPS = 1e-6
POOL_WINDOWS = (2, 4, 8, 16)
MLSTM_HEADS = 8
CONV_K = 4

SUBLANES = 8
LANES = 128
VMEM_LIMIT_BYTES = 56 * 1024 * 1024

POOL_SEQ_TILE = 1024
POOL_HALO = 16
MOE_TOKEN_TILE = 1024
MOE_EXPERTS_PER_STEP = 4
ROUTER_LO_LANE = 32
MLSTM_CHUNK = 256
MLSTM_BATCH_PER_STEP = 4
CONV_CARRY = SUBLANES
STATE_EXTRA_ROWS = 16

F32 = jnp.float32
BF16 = jnp.bfloat16
_DONE = object()


def _rmsnorm(v, g):
    return (v * lax.rsqrt(jnp.mean(v * v, axis=-1, keepdims=True) + EPS)) * g


def _sigmoid(v):
    return 1.0 / (1.0 + jnp.exp(-v))


def _log_sigmoid(v):
    return jnp.minimum(v, 0.0) - jnp.log1p(jnp.exp(-jnp.abs(v)))


def _split_bf16(a, parts):
    out = []
    for _ in range(parts - 1):
        hi = a.astype(BF16)
        out.append(hi)
        a = a - hi.astype(F32)
    out.append(a.astype(BF16))
    return out


def _side_cast_specs(cast, grid):
    if cast is None:
        return [], [], [], []
    layer, arrays = cast
    n_steps = grid[0] * grid[1]
    in_specs, out_specs, out_shapes = [], [], []
    for w in arrays:
        n_e = w.shape[1]
        assert n_steps % n_e == 0
        slabs = n_steps // n_e
        rows = w.shape[2] // slabs
        assert rows * slabs == w.shape[2] and rows % (2 * SUBLANES) == 0
        blk = (1, 1, rows, w.shape[3])
        slab_of = lambda i, j, slabs=slabs: ((i * grid[1] + j) // slabs, (i * grid[1] + j) % slabs)
        in_specs.append(pl.BlockSpec(blk, lambda i, j, f=slab_of: (layer, *f(i, j), 0)))
        out_specs.append(pl.BlockSpec(blk, lambda i, j, f=slab_of: (0, *f(i, j), 0)))
        out_shapes.append(jax.ShapeDtypeStruct((1,) + w.shape[1:], BF16))
    return list(arrays), in_specs, out_specs, out_shapes


def _side_cast(srcs, dsts):
    for src, dst in zip(srcs, dsts):
        dst[...] = src[...].astype(BF16)


def _pool_kernel(x_ref, halo_ref, g_ref, w_ref, scale_ref, *rest):
    n_cast = len(rest) // 2
    o_ref = rest[n_cast]
    _side_cast(rest[:n_cast], rest[n_cast + 1:])
    j = pl.program_id(1)
    ts = x_ref.shape[1]
    gc = w_ref.shape[1]
    x = x_ref[0]
    g = g_ref[...]
    hn = _rmsnorm(x, g)
    hh = _rmsnorm(halo_ref[0], g) * (j > 0).astype(F32)
    s = jnp.concatenate([hh, hn], axis=0)
    sums = []
    shift = 1
    for _ in POOL_WINDOWS:
        s = s + pltpu.roll(s, shift, 0)
        sums.append(s[POOL_HALO:, :gc])
        s = s[:, gc:]
        shift *= 2
    t = (j * ts + lax.broadcasted_iota(jnp.int32, (ts, 1), 0) + 1).astype(F32)
    ys = []
    for k, w in enumerate(POOL_WINDOWS):
        count = jnp.minimum(t, float(w))
        pooled = sums[k] / count - hn[:, k * gc:(k + 1) * gc]
        ys.append(jnp.dot(pooled.astype(BF16), w_ref[k], preferred_element_type=F32))
    y = jnp.concatenate(ys, axis=-1)
    o_ref[0] = x + y * scale_ref[...]


def _pool_layer(x, g, w, scale, cast=None):
    B, S, D = x.shape
    ts = min(POOL_SEQ_TILE, S)
    assert S % ts == 0 and ts % POOL_HALO == 0
    n_groups, gc, _ = w.shape
    assert n_groups == len(POOL_WINDOWS) and n_groups * gc == D
    halo_blocks = ts // POOL_HALO
    cast_arrays, cast_in_specs, cast_out_specs, cast_out_shapes = _side_cast_specs(cast, (B, S // ts))
    out, *casted = pl.pallas_call(
        _pool_kernel,
        grid=(B, S // ts),
        in_specs=[
            pl.BlockSpec((1, ts, D), lambda b, j: (b, j, 0)),
            pl.BlockSpec((1, POOL_HALO, D), lambda b, j: (b, jnp.maximum(j * halo_blocks - 1, 0), 0)),
            pl.BlockSpec((1, D), lambda b, j: (0, 0)),
            pl.BlockSpec((n_groups, gc, gc), lambda b, j: (0, 0, 0)),
            pl.BlockSpec((1, D), lambda b, j: (0, 0)),
        ] + cast_in_specs,
        out_specs=[pl.BlockSpec((1, ts, D), lambda b, j: (b, j, 0))] + cast_out_specs,
        out_shape=[jax.ShapeDtypeStruct((B, S, D), F32)] + cast_out_shapes,
        compiler_params=pltpu.CompilerParams(
            dimension_semantics=("arbitrary", "arbitrary"), vmem_limit_bytes=VMEM_LIMIT_BYTES),
        name="pool_mixer",
    )(x, x, g.reshape(1, D), w.astype(BF16), scale.reshape(1, D), *cast_arrays)
    return out, casted


def _route_t(logits_t, bgr, ber):
    n_e, n_g = ber.shape[0], bgr.shape[0]
    el = logits_t[:n_e, :] + ber
    gl = logits_t[n_e:n_e + n_g, :] + bgr
    gidx = lax.broadcasted_iota(jnp.int32, gl.shape, 0)
    gmax = jnp.max(gl, axis=0, keepdims=True)
    gsel = jnp.min(jnp.where(gl == gmax, gidx, n_g), axis=0, keepdims=True)
    pg = 1.0 / jnp.sum(jnp.exp(gl - gmax), axis=0, keepdims=True)
    eidx = lax.broadcasted_iota(jnp.int32, el.shape, 0)
    in_group = (eidx // (n_e // n_g)) == gsel
    neg_inf = jnp.float32(-jnp.inf)
    cand = jnp.where(in_group, el, neg_inf)
    v1 = jnp.max(cand, axis=0, keepdims=True)
    i1 = jnp.min(jnp.where(cand == v1, eidx, n_e), axis=0, keepdims=True)
    cand2 = jnp.where(eidx == i1, neg_inf, cand)
    v2 = jnp.max(cand2, axis=0, keepdims=True)
    i2 = jnp.min(jnp.where(cand2 == v2, eidx, n_e), axis=0, keepdims=True)
    r = jnp.exp(v2 - v1)
    w1 = pg / (1.0 + r)
    w2 = pg * r / (1.0 + r)
    return jnp.where(eidx == i1, w1, 0.0) + jnp.where(eidx == i2, w2, 0.0)


def _moe_kernel(h_ref, g_ref, wrc_ref, wrh_ref, bgr_ref, ber_ref, wg_ref, wu_ref, wd_ref, fn_ref, *rest,
                final_norm, n_cast):
    cast_in, (o_ref, *cast_out), (hn_ref, comb_ref) = rest[:n_cast], rest[n_cast:2 * n_cast + 1], rest[2 * n_cast + 1:]
    _side_cast(cast_in, cast_out)
    step = pl.program_id(1)
    eps = wg_ref.shape[1]
    tm = h_ref.shape[0]
    n_f, n_d = wd_ref.shape[2:]

    def experts(hn, comb):
        eidx = lax.broadcasted_iota(jnp.int32, comb.shape, 1)
        acts, y = [], None
        for jj in range(eps + 1):
            if jj < eps:
                comb_e = jnp.sum(jnp.where(eidx == step * eps + jj, comb, 0.0), axis=-1, keepdims=True)
                hg = jnp.dot(hn, wg_ref[0, jj], preferred_element_type=F32)
                hu = jnp.dot(hn, wu_ref[0, jj], preferred_element_type=F32)
                acts.append(((hg * _sigmoid(hg)) * hu * comb_e).astype(BF16))
            if jj > 0:
                d = jnp.dot(acts[jj - 1], wd_ref[0, jj - 1], preferred_element_type=F32)
                y = d if y is None else y + d
        return y

    @pl.when(step == 0)
    def _():
        x = h_ref[...]
        hn, hn_lo = _split_bf16(_rmsnorm(x, g_ref[...]), 2)
        logits_t = (jnp.dot(hn, wrc_ref[...], preferred_element_type=F32)
                    + jnp.dot(hn_lo, wrh_ref[...], preferred_element_type=F32)).T
        n_l = ROUTER_LO_LANE
        comb_t = _route_t(logits_t[:n_l, :] + logits_t[n_l:2 * n_l, :], bgr_ref[...], ber_ref[...])
        comb_t = jnp.concatenate([comb_t, jnp.zeros((LANES - comb_t.shape[0], tm), F32)], axis=0)
        comb = comb_t.T
        comb_ref[...] = comb
        hn_ref[...] = hn
        o_ref[...] = x + experts(hn, comb)

    @pl.when(step > 0)
    def _():
        o_ref[...] += experts(hn_ref[...], comb_ref[...])

    if final_norm:
        @pl.when(step == pl.num_programs(1) - 1)
        def _():
            o_ref[...] = _rmsnorm(o_ref[...], fn_ref[...])


def _moe_layer(h, g, wgr, bgr, wer, ber, wg, wu, wd, fn=None, cast_next=None):
    T, D = h.shape
    _, n_e, _, F = wg.shape
    n_g = wgr.shape[-1]
    tm = min(MOE_TOKEN_TILE, T)
    eps = MOE_EXPERTS_PER_STEP
    assert T % tm == 0 and n_e % eps == 0 and n_e + n_g <= ROUTER_LO_LANE
    n_inner = n_e // eps
    final_norm = fn is not None
    if fn is None:
        fn = jnp.ones((D,), F32)
    cast_arrays, cast_in_specs, cast_out_specs, cast_out_shapes = _side_cast_specs(cast_next, (T // tm, n_inner))
    wr_hi, wr_lo = _split_bf16(jnp.concatenate([wer, wgr], axis=1), 2)
    pad = lambda a, n: jnp.pad(a, ((0, 0), (0, n - a.shape[1])))
    wr_cat = jnp.concatenate([pad(wr_hi, ROUTER_LO_LANE), pad(wr_lo, LANES - ROUTER_LO_LANE)], axis=1)
    const2 = lambda i, e: (0, 0)
    full = lambda a: pl.BlockSpec(a.shape, const2)
    small = [g.reshape(1, D), wr_cat, pad(wr_hi, LANES), bgr.reshape(n_g, 1), ber.reshape(n_e, 1)]
    out, *cast = pl.pallas_call(
        functools.partial(_moe_kernel, final_norm=final_norm, n_cast=len(cast_arrays)),
        grid=(T // tm, n_inner),
        in_specs=[pl.BlockSpec((tm, D), lambda i, e: (i, 0))] + [full(a) for a in small] + [
            pl.BlockSpec((1, eps, D, F), lambda i, e: (0, e, 0, 0)),
            pl.BlockSpec((1, eps, D, F), lambda i, e: (0, e, 0, 0)),
            pl.BlockSpec((1, eps, F, D), lambda i, e: (0, e, 0, 0)),
            pl.BlockSpec((1, D), const2),
        ] + cast_in_specs,
        out_specs=[pl.BlockSpec((tm, D), lambda i, e: (i, 0))] + cast_out_specs,
        out_shape=[jax.ShapeDtypeStruct((T, D), F32)] + cast_out_shapes,
        scratch_shapes=[pltpu.VMEM((tm, D), BF16), pltpu.VMEM((tm, LANES), F32)],
        compiler_params=pltpu.CompilerParams(
            dimension_semantics=("arbitrary", "arbitrary"), vmem_limit_bytes=VMEM_LIMIT_BYTES),
        name="moe_final" if final_norm else "moe",
    )(h, *small, wg, wu, wd, fn.reshape(1, D), *cast_arrays)
    return out, cast


def _mlstm_kernel(h_ref, g_ref, wqk_ref, wv_t_ref, wo_t_ref, wgate_t_ref, convw_ref, convb_ref,
                  bi_t_ref, bf_t_ref, hnorm_ref, wout_t_ref, o_ref,
                  carry_ref, c_ref, m_ref):
    @pl.when(pl.program_id(1) == 0)
    def _():
        carry_ref[...] = jnp.zeros_like(carry_ref)
        c_ref[...] = jnp.zeros_like(c_ref)
        m_ref[...] = jnp.zeros_like(m_ref)

    rows = [_mlstm_chunk(bb, h_ref, g_ref, wqk_ref, wv_t_ref, wo_t_ref, wgate_t_ref, convw_ref, convb_ref,
                         bi_t_ref, bf_t_ref, hnorm_ref, wout_t_ref, o_ref, carry_ref, c_ref, m_ref)
            for bb in range(h_ref.shape[0])]
    while rows:
        rows = [r for r in rows if next(r, _DONE) is not _DONE]


def _mlstm_chunk(bb, h_ref, g_ref, wqk_ref, wv_t_ref, wo_t_ref, wgate_t_ref, convw_ref, convb_ref,
                 bi_t_ref, bf_t_ref, hnorm_ref, wout_t_ref, o_ref, carry_ref, c_ref, m_ref):
    L = h_ref.shape[1]
    H = MLSTM_HEADS
    qk2 = convw_ref.shape[1]
    v_w = wout_t_ref.shape[1]
    dqk = qk2 // (2 * H)
    dv = v_w // H
    pw = 2 * dv
    ext_rows = c_ref.shape[2]
    assert 2 * dqk == LANES and dv == LANES and ext_rows == pw + STATE_EXTRA_ROWS
    nt = (((1,), (1,)), ((), ()))

    x = h_ref[bb]
    hn = _rmsnorm(x, g_ref[...])
    hb = hn.astype(BF16)
    gates_t = lax.dot_general(wgate_t_ref[...], hb, nt, preferred_element_type=F32)
    gates_t = gates_t[:2 * H, :] + gates_t[2 * H:, :]
    li_t = gates_t[:H, :] + bi_t_ref[...]
    lf_t = _log_sigmoid(gates_t[H:, :] + bf_t_ref[...])
    key = lax.broadcasted_iota(jnp.int32, (L, L), 0)
    qry = lax.broadcasted_iota(jnp.int32, (L, L), 1)
    causal_t = key <= qry
    b_t = sum(jnp.dot(part, causal_t.astype(BF16), preferred_element_type=F32)
              for part in _split_bf16(lf_t, 3))
    r_t = b_t - li_t
    r = jnp.concatenate([r_t, jnp.zeros((LANES - H, L), F32)], axis=0).T

    yield
    lane_t = lax.broadcasted_iota(jnp.int32, (1, L), 1)
    neg_inf = jnp.float32(-jnp.inf)
    best = li_t - b_t
    shift = 1
    while shift < L:
        best = jnp.maximum(best, jnp.where(lane_t >= shift, pltpu.roll(best, shift, 1), neg_inf))
        shift *= 2
    m_prev = m_ref[bb]
    inter_t = b_t + m_prev
    m_t = jnp.maximum(inter_t, b_t + best)
    s_inter_t = jnp.exp(inter_t - m_t)
    u_t = b_t - m_t
    clamp_t = jnp.exp(-m_t)
    b_last = b_t[:, L - 1:L]
    gsum_t = b_last - b_t + li_t
    m_new = jnp.maximum(b_last + m_prev, jnp.max(gsum_t, axis=1, keepdims=True))
    w_s_t = jnp.exp(gsum_t - m_new)
    decay = jnp.exp(b_last + m_prev - m_new)
    m_ref[bb] = m_new

    yield
    first = lax.broadcasted_iota(jnp.int32, (1, LANES), 1) < dqk
    erow = lax.broadcasted_iota(jnp.int32, (ext_rows, 1), 0)
    rows_a = (erow < dv) | (erow == pw)
    rows_b = ((erow >= dv) & (erow < pw)) | (erow == pw + 1)
    own_lanes = (rows_a & first) | (rows_b & jnp.logical_not(first))
    xrow = lax.broadcasted_iota(jnp.int32, (STATE_EXTRA_ROWS, 2 * L), 0)
    xcol = lax.broadcasted_iota(jnp.int32, (STATE_EXTRA_ROWS, 2 * L), 1)
    ones_rows = (((xrow == 0) & (xcol < L)) | ((xrow == 1) & (xcol >= L))).astype(BF16)
    zeros_t = jnp.zeros((dv, L), BF16)
    zeros_x = jnp.zeros((STATE_EXTRA_ROWS - 2, L), F32)
    projs = []

    def pair(p):
        heads = (2 * p, 2 * p + 1)
        qk_cols = slice(p * pw, (p + 1) * pw)
        qk_pre = jnp.dot(hb, wqk_ref[:, qk_cols], preferred_element_type=F32)
        v_t = lax.dot_general(wv_t_ref[qk_cols, :], hb, nt, preferred_element_type=F32)
        o_t = lax.dot_general(wo_t_ref[qk_cols, :], hb, nt, preferred_element_type=F32)

        yield
        ext = jnp.concatenate([carry_ref[bb, :, qk_cols], qk_pre], axis=0)
        carry_ref[bb, :, qk_cols] = qk_pre[L - CONV_CARRY:, :]
        conv = ext * convw_ref[CONV_K - 1:CONV_K, qk_cols]
        for d in range(1, CONV_K):
            conv = conv + pltpu.roll(ext, d, 0) * convw_ref[CONV_K - 1 - d:CONV_K - d, qk_cols]
        qk = conv[CONV_CARRY:, :] + convb_ref[:, qk_cols]
        qk = qk * _sigmoid(qk)
        q2 = qk[:, :LANES].astype(BF16)
        k2 = (qk[:, LANES:] * (dqk ** -0.5)).astype(BF16)

        yield
        c_ext = c_ref[bb, p]
        k_stack = jnp.concatenate([jnp.where(first, k2, jnp.zeros_like(k2)),
                                   jnp.where(first, jnp.zeros_like(k2), k2)], axis=0)
        s_t = lax.dot_general(k_stack, q2, nt, preferred_element_type=F32)
        inter_ext = lax.dot_general(c_ext.astype(BF16), q2, nt, preferred_element_type=F32)
        yield
        scs = []
        for half, hh in enumerate(heads):
            p_t = jnp.exp(jnp.where(causal_t, u_t[hh:hh + 1, :] - r[:, hh:hh + 1], neg_inf))
            scs.append((s_t[half * L:(half + 1) * L, :] * p_t).astype(BF16))
            yield
        vb = v_t.astype(BF16)
        v_ext = jnp.concatenate([jnp.concatenate([vb[:dv], zeros_t], axis=1),
                                 jnp.concatenate([zeros_t, vb[dv:]], axis=1), ones_rows], axis=0)
        num_ext = jnp.dot(v_ext, jnp.concatenate(scs, axis=0), preferred_element_type=F32)
        yield
        outs = []
        for half, hh in enumerate(heads):
            hs = slice(half * dv, (half + 1) * dv)
            si = s_inter_t[hh:hh + 1, :]
            num = num_ext[hs] + si * inter_ext[hs]
            den = num_ext[pw + half:pw + half + 1] + si * inter_ext[pw + half:pw + half + 1]
            ho = num * (1.0 / jnp.maximum(jnp.abs(den), clamp_t[hh:hh + 1, :]))
            outs.append(ho * lax.rsqrt(jnp.mean(ho * ho, axis=0, keepdims=True) + EPS))
            yield
        hnorm = hnorm_ref[qk_cols, :]
        gated_t = (jnp.concatenate(outs, axis=0) * jnp.concatenate([hnorm] * (L // LANES), axis=1)
                   * _sigmoid(o_t))
        projs.append(jnp.dot(wout_t_ref[:, qk_cols], gated_t.astype(BF16), preferred_element_type=F32))

        yield
        ws_a, ws_b = (w_s_t[hh:hh + 1, :] for hh in heads)
        wsv_ext = jnp.concatenate([ws_a * v_t[:dv], ws_b * v_t[dv:], ws_a, ws_b, zeros_x], axis=0).astype(BF16)
        upd = jnp.dot(wsv_ext, k2, preferred_element_type=F32)
        d_a, d_b = (decay[hh:hh + 1, :] for hh in heads)
        c_ref[bb, p] = jnp.where(rows_a, d_a, d_b) * c_ext + jnp.where(own_lanes, upd, 0.0)

    pairs = [pair(p) for p in range(H // 2)]
    while pairs:
        pairs = [g for g in pairs if next(g, _DONE) is not _DONE]
        yield
    o_ref[bb] = x + sum(projs).T


def _pair_major(a, n_pairs):
    half = a.shape[-1] // 2
    blk = half // n_pairs
    parts = []
    for p in range(n_pairs):
        parts += [a[..., p * blk:(p + 1) * blk], a[..., half + p * blk:half + (p + 1) * blk]]
    return jnp.concatenate(parts, axis=-1)


def _mlstm_layer(h, g, w_in, conv_w, conv_b, b_i, b_f, head_norm, w_out):
    B, S, D = h.shape
    H = MLSTM_HEADS
    v_w = w_out.shape[0]
    qk2 = conv_w.shape[1]
    assert w_in.shape[1] == qk2 + 2 * v_w + 2 * H and qk2 == v_w
    L = min(MLSTM_CHUNK, S)
    nb = MLSTM_BATCH_PER_STEP
    assert S % L == 0 and B % nb == 0 and L % LANES == 0
    w_gate = jnp.concatenate(_split_bf16(w_in[:, qk2 + 2 * v_w:], 2), axis=1)
    const2 = lambda b, j: (0, 0)
    full = lambda a: pl.BlockSpec(a.shape, const2)
    operands = [g.reshape(1, D), _pair_major(w_in[:, :qk2], H // 2).astype(BF16),
                w_in[:, qk2:qk2 + v_w].T.astype(BF16), w_in[:, qk2 + v_w:qk2 + 2 * v_w].T.astype(BF16),
                w_gate.T, _pair_major(conv_w, H // 2), _pair_major(conv_b.reshape(1, qk2), H // 2),
                b_i.reshape(H, 1), b_f.reshape(H, 1),
                jnp.broadcast_to(head_norm.reshape(v_w, 1), (v_w, LANES)), w_out.T.astype(BF16)]
    return pl.pallas_call(
        _mlstm_kernel,
        grid=(B // nb, S // L),
        in_specs=[pl.BlockSpec((nb, L, D), lambda b, j: (b, j, 0))] + [full(a) for a in operands],
        out_specs=pl.BlockSpec((nb, L, D), lambda b, j: (b, j, 0)),
        out_shape=jax.ShapeDtypeStruct((B, S, D), F32),
        scratch_shapes=[
            pltpu.VMEM((nb, CONV_CARRY, qk2), F32),
            pltpu.VMEM((nb, H // 2, 2 * v_w // H + STATE_EXTRA_ROWS, LANES), F32),
            pltpu.VMEM((nb, H, 1), F32),
        ],
        compiler_params=pltpu.CompilerParams(
            dimension_semantics=("arbitrary", "arbitrary"), vmem_limit_bytes=VMEM_LIMIT_BYTES),
        name="mlstm_mixer",
    )(h, *operands)


def kernel(x, mix_norm, pool_w, pool_scale, w_in, conv_w, conv_b, gate_bias_i, gate_bias_f, head_norm,
           w_out, ffn_norm, w_group_router, b_group_router, w_expert_router, b_expert_router, w_gate,
           w_up, w_down, final_norm):
    B, S, D = x.shape
    depth = mix_norm.shape[0]
    n_mixers = 2
    expert_w = (w_gate, w_up, w_down)
    expert_w_b = None
    h = x
    for i in range(depth):
        j = i // n_mixers
        if i % n_mixers == 0:
            h, casted = _pool_layer(h, mix_norm[i], pool_w[j], pool_scale[j],
                                    (i, expert_w) if expert_w_b is None else None)
            expert_w_b = expert_w_b or casted
        else:
            h = _mlstm_layer(h, mix_norm[i], w_in[j], conv_w[j], conv_b[j], gate_bias_i[j],
                             gate_bias_f[j], head_norm[j], w_out[j])
            expert_w_b = expert_w_b or [w[i:i + 1].astype(BF16) for w in expert_w]
        fn = final_norm if i == depth - 1 else None
        cast_next = (i + 1, expert_w) if i + 1 < depth else None
        h, expert_w_b = _moe_layer(h.reshape(B * S, D), ffn_norm[i], w_group_router[i], b_group_router[i],
                                   w_expert_router[i], b_expert_router[i], *expert_w_b, fn, cast_next)
        h = h.reshape(B, S, D)
    return h
```
